```python
import math
import numpy as np
import jax
import jax.numpy as jnp
from jax import lax

D_MODEL = 1024
BATCH = 4
SEQ = 8192
DEPTH = 2

GRID_W = 64
CTX_LEN = 256
WIN_HEADS = 8
WIN_KV_HEADS = 2
WIN_HEAD_DIM = 64
WINDOW = 128
BLOCK = 128
HG_HEADS = 4
HG_KEY_DIM = 64
HG_VAL_DIM = 64
CHUNK = 64
DIFF_HEADS = 4
DIFF_QK_DIM = 32
DIFF_V_DIM = 64
MIX_WIDTH = WIN_HEADS * WIN_HEAD_DIM + HG_HEADS * HG_VAL_DIM + DIFF_HEADS * DIFF_V_DIM
D_FF = 11 * D_MODEL // 4
CONV_WIDTH = 3
ROPE_BASE = 10000.0
EPS = 1e-6
SPLIT_SIZES = (
    WIN_HEADS * WIN_HEAD_DIM, WIN_KV_HEADS * WIN_HEAD_DIM, WIN_KV_HEADS * WIN_HEAD_DIM,
    HG_HEADS * HG_KEY_DIM, HG_HEADS * HG_KEY_DIM,
    HG_HEADS * HG_KEY_DIM, HG_HEADS * HG_KEY_DIM,
    HG_HEADS * HG_VAL_DIM, HG_HEADS * HG_VAL_DIM,
    DIFF_HEADS * 2 * DIFF_QK_DIM, DIFF_HEADS * 2 * DIFF_QK_DIM, DIFF_HEADS * DIFF_V_DIM,
)
IN_WIDTH = sum(SPLIT_SIZES)

kernel_name = 'hymba_style_hybrid_diffusion_trunk'

F32 = jnp.float32


def rms_norm(x, g):
    xf = x.astype(F32)
    y = xf * lax.rsqrt(jnp.mean(xf * xf, axis=-1, keepdims=True) + EPS)
    return (y * g.astype(F32)).astype(x.dtype)


def split_columns(p):
    idx = np.cumsum(SPLIT_SIZES)[:-1].tolist()
    return jnp.split(p, idx, axis=-1)


def adaln(vec, w, b):
    m = jax.nn.silu(vec) @ w + b
    return [t.reshape(-1, 1, D_MODEL) for t in jnp.split(m, 6, axis=-1)]


def axial_rope_tables(L, dim):
    rows = L // GRID_W
    row = jnp.repeat(jnp.arange(rows, dtype=F32), GRID_W)
    col = jnp.tile(jnp.arange(GRID_W, dtype=F32), rows)
    axis_dim = dim // 2
    n = axis_dim // 2
    inv = jnp.power(ROPE_BASE, -jnp.arange(n, dtype=F32) * 2.0 / axis_dim)
    ar = row[:, None] * inv[None, :]
    ac = col[:, None] * inv[None, :]
    return (jnp.cos(ar), jnp.sin(ar), jnp.cos(ac), jnp.sin(ac))


def _rotate_half(x, cos, sin):
    x1, x2 = jnp.split(x, 2, axis=-1)
    return jnp.concatenate([x1 * cos - x2 * sin, x1 * sin + x2 * cos], axis=-1)


def apply_axial_rope(x, tables):
    cos_r, sin_r, cos_c, sin_c = tables
    shape = (x.shape[1],) + (1,) * (x.ndim - 3) + (cos_r.shape[-1],)
    xr, xcol = jnp.split(x.astype(F32), 2, axis=-1)
    out = jnp.concatenate([
        _rotate_half(xr, cos_r.reshape(shape), sin_r.reshape(shape)),
        _rotate_half(xcol, cos_c.reshape(shape), sin_c.reshape(shape))], axis=-1)
    return out.astype(x.dtype)


def softmax_with_sink(s, sink):
    m = jnp.maximum(jnp.max(s, axis=-1, keepdims=True), sink)
    e = jnp.exp(s - m)
    return e / (jnp.sum(e, axis=-1, keepdims=True) + jnp.exp(sink - m))


def window_gqa_latent(q, k, v, kc, vc, sink):
    B, L, H, d = q.shape
    G = H // WIN_KV_HEADS
    nb = L // BLOCK
    qg = q.reshape(B, L, WIN_KV_HEADS, G, d)
    pad = ((0, 0), (BLOCK, BLOCK), (0, 0), (0, 0))
    kp = jnp.pad(k, pad)
    vp = jnp.pad(v, pad)
    sink = sink.astype(F32).reshape(1, WIN_KV_HEADS, G, 1, 1)
    scale = d ** -0.5

    def block(j):
        start = j * BLOCK
        qb = lax.dynamic_slice_in_dim(qg, start, BLOCK, axis=1)
        kb = lax.dynamic_slice_in_dim(kp, start, 3 * BLOCK, axis=1)
        vb = lax.dynamic_slice_in_dim(vp, start, 3 * BLOCK, axis=1)
        qpos = start + jnp.arange(BLOCK)
        kpos = start - BLOCK + jnp.arange(3 * BLOCK)
        band = (jnp.abs(qpos[:, None] - kpos[None, :]) <= WINDOW) & (kpos >= 0)[None, :] & (kpos < L)[None, :]
        s_loc = jnp.einsum('bqkgd,bskd->bkgqs', qb, kb).astype(F32) * scale
        s_loc = jnp.where(band, s_loc, -jnp.inf)
        s_ctx = jnp.einsum('bqkgd,bskd->bkgqs', qb, kc).astype(F32) * scale
        p = softmax_with_sink(jnp.concatenate([s_loc, s_ctx], axis=-1), sink).astype(v.dtype)
        o = (jnp.einsum('bkgqs,bskd->bqkgd', p[..., :3 * BLOCK], vb)
             + jnp.einsum('bkgqs,bskd->bqkgd', p[..., 3 * BLOCK:], vc))
        return o.reshape(B, BLOCK, H * d)

    o = lax.map(block, jnp.arange(nb))
    return o.transpose(1, 0, 2, 3).reshape(B, L, H * d)


def gqa_context(qc, kc, vc, sink):
    B, Lc, H, d = qc.shape
    G = H // WIN_KV_HEADS
    qg = qc.reshape(B, Lc, WIN_KV_HEADS, G, d)
    s = jnp.einsum('bqkgd,bskd->bkgqs', qg, kc).astype(F32) * d ** -0.5
    p = softmax_with_sink(s, sink.astype(F32).reshape(1, WIN_KV_HEADS, G, 1, 1)).astype(vc.dtype)
    return jnp.einsum('bkgqs,bskd->bqkgd', p, vc).reshape(B, Lc, H * d)


def hgrn2_lower_bounds(raw):
    p = jax.nn.softmax(raw.astype(F32), axis=0)
    return jnp.cumsum(p, axis=0) - p[0]


def hgrn2_gates(pq, pf, lb):
    B, L, _ = pq.shape
    q = pq.reshape(B, L, HG_HEADS, HG_KEY_DIM).astype(F32)
    lb = lb.reshape(HG_HEADS, HG_KEY_DIM)
    f = lb + (1.0 - lb) * jax.nn.sigmoid(pf.reshape(B, L, HG_HEADS, HG_KEY_DIM).astype(F32))
    return q, jnp.log(f), 1.0 - f


def hgrn2_chunk_scan(q, logf, k, v, s0, with_output):
    B, L, H, dk = q.shape
    dv = v.shape[-1]
    n = L // CHUNK

    def chunks(t):
        return t.reshape(B, n, CHUNK, H, t.shape[-1]).transpose(1, 0, 3, 2, 4)

    tri = jnp.tril(jnp.ones((CHUNK, CHUNK), dtype=bool))[None, None, :, :, None]

    def step(S, xs):
        qc, lfc, kc, vc = xs
        b = jnp.cumsum(lfc, axis=2)
        b_end = b[:, :, -1]
        S_new = (jnp.exp(b_end)[..., None] * S
                 + jnp.einsum('bhsk,bhsv->bhkv', kc * jnp.exp(b_end[:, :, None, :] - b), vc))
        if not with_output:
            return S_new, None
        rel = jnp.where(tri, b[:, :, :, None, :] - b[:, :, None, :, :], -jnp.inf)
        A = jnp.einsum('bhtk,bhsk,bhtsk->bhts', qc, kc, jnp.exp(rel))
        o = (jnp.einsum('bhts,bhsv->bhtv', A, vc)
             + jnp.einsum('bhtk,bhkv->bhtv', qc * jnp.exp(b), S))
        return S_new, o

    S, o = lax.scan(step, s0, (chunks(q), chunks(logf), chunks(k), chunks(v)))
    if not with_output:
        return S, None
    return S, o.transpose(1, 0, 3, 2, 4).reshape(B, L, H, dv)


def maybe_flip(t, rev):
    return jnp.flip(t, axis=1) if rev else t


def hgrn2_bidirectional(pl, pc, lb, need_ctx):
    B, L, _ = pl[4].shape
    Lc = pc[4].shape[1]
    il = pl[4].reshape(B, L, HG_HEADS, HG_VAL_DIM).astype(F32)
    ic = pc[4].reshape(B, Lc, HG_HEADS, HG_VAL_DIM).astype(F32)
    o_l = 0.0
    o_c = 0.0
    for d in range(2):
        rev = d == 1
        ql, lfl, kl = hgrn2_gates(pl[2 * d], pl[2 * d + 1], lb[d])
        qc, lfc, kc = hgrn2_gates(pc[2 * d], pc[2 * d + 1], lb[d])
        s0 = jnp.zeros((B, HG_HEADS, HG_KEY_DIM, HG_VAL_DIM), F32)
        s_ctx, oc = hgrn2_chunk_scan(maybe_flip(qc, rev), maybe_flip(lfc, rev), maybe_flip(kc, rev),
                                     maybe_flip(ic, rev), s0, need_ctx)
        _, ol = hgrn2_chunk_scan(maybe_flip(ql, rev), maybe_flip(lfl, rev), maybe_flip(kl, rev),
                                 maybe_flip(il, rev), s_ctx, True)
        o_l = o_l + maybe_flip(ol, rev)
        if need_ctx:
            o_c = o_c + maybe_flip(oc, rev)
    return o_l, (o_c if need_ctx else None)


def hgrn2_readout(o, g, og):
    B, L = o.shape[:2]
    gate = jax.nn.silu(g.reshape(B, L, HG_HEADS, HG_VAL_DIM).astype(F32))
    return (rms_norm(o, og) * gate).reshape(B, L, HG_HEADS * HG_VAL_DIM)


def diff_attn(q, k, v, lam):
    s = jnp.einsum('bqhcd,bshcd->bhcqs', q, k).astype(F32) * DIFF_QK_DIM ** -0.5
    p = jax.nn.softmax(s, axis=-1)
    a = (p[:, :, 0] - lam * p[:, :, 1]).astype(v.dtype)
    return jnp.einsum('bhqs,bshd->bqhd', a, v)


def diff_attn_latent(q, k_all, v_all, lam):
    B, L = q.shape[:2]
    nb = L // BLOCK

    def block(j):
        qb = lax.dynamic_slice_in_dim(q, j * BLOCK, BLOCK, axis=1)
        return diff_attn(qb, k_all, v_all, lam)

    o = lax.map(block, jnp.arange(nb))
    return o.transpose(1, 0, 2, 3, 4).reshape(B, L, DIFF_HEADS, DIFF_V_DIM)


def token_mixers(hl, hc, w_in, win_qg, win_kg, win_sink, lb, hg_og, diff_qg, diff_kg, diff_lam, diff_og,
                 lam_init, rope_a, rope_d, need_ctx):
    B, L, _ = hl.shape
    Lc = hc.shape[1]
    pl = split_columns(hl @ w_in)
    pc = split_columns(hc @ w_in)

    def a_qkv(p, n):
        q = rms_norm(p[0].reshape(B, n, WIN_HEADS, WIN_HEAD_DIM), win_qg)
        k = rms_norm(p[1].reshape(B, n, WIN_KV_HEADS, WIN_HEAD_DIM), win_kg)
        v = p[2].reshape(B, n, WIN_KV_HEADS, WIN_HEAD_DIM)
        return q, k, v

    qa, ka, va = a_qkv(pl, L)
    qa = apply_axial_rope(qa, rope_a)
    ka = apply_axial_rope(ka, rope_a)
    qac, kac, vac = a_qkv(pc, Lc)
    oa_l = window_gqa_latent(qa, ka, va, kac, vac, win_sink)

    ob_l, ob_c = hgrn2_bidirectional(pl[3:8], pc[3:8], lb, need_ctx)
    ob_l = hgrn2_readout(ob_l, pl[8], hg_og).astype(hl.dtype)

    lam_f = diff_lam.astype(F32)
    lam = jnp.exp(jnp.sum(lam_f[0] * lam_f[1])) - jnp.exp(jnp.sum(lam_f[2] * lam_f[3])) + lam_init

    def d_qkv(p, n):
        q = rms_norm(p[9].reshape(B, n, DIFF_HEADS, 2, DIFF_QK_DIM), diff_qg)
        k = rms_norm(p[10].reshape(B, n, DIFF_HEADS, 2, DIFF_QK_DIM), diff_kg)
        v = p[11].reshape(B, n, DIFF_HEADS, DIFF_V_DIM)
        return q, k, v

    qd, kd, vd = d_qkv(pl, L)
    qd = apply_axial_rope(qd, rope_d)
    kd = apply_axial_rope(kd, rope_d)
    qdc, kdc, vdc = d_qkv(pc, Lc)
    od_l = diff_attn_latent(qd, jnp.concatenate([kd, kdc], axis=1), jnp.concatenate([vd, vdc], axis=1), lam)
    od_l = (rms_norm(od_l, diff_og) * (1.0 - lam_init)).reshape(B, L, DIFF_HEADS * DIFF_V_DIM)

    out_l = jnp.concatenate([oa_l, ob_l, od_l.astype(hl.dtype)], axis=-1)
    if not need_ctx:
        return out_l, None

    oa_c = gqa_context(qac, kac, vac, win_sink)
    ob_c = hgrn2_readout(ob_c, pc[8], hg_og).astype(hc.dtype)
    od_c = (rms_norm(diff_attn(qdc, kdc, vdc, lam), diff_og) * (1.0 - lam_init)).reshape(B, Lc, -1)
    out_c = jnp.concatenate([oa_c, ob_c, od_c.astype(hc.dtype)], axis=-1)
    return out_l, out_c


def conv_ffn(h, w_up, conv_w, conv_b, w_down):
    u = h @ w_up
    L = u.shape[1]
    r = CONV_WIDTH // 2
    up = jnp.pad(u, ((0, 0), (r, r), (0, 0)))
    y = conv_b + up[:, 0:L] * conv_w[0]
    for j in range(1, CONV_WIDTH):
        y = y + up[:, j:j + L] * conv_w[j]
    a, val = jnp.split(y, 2, axis=-1)
    return (jax.nn.silu(a) * val) @ w_down


def setup_inputs(seed: int = 0) -> dict:
    key = jax.random.key(seed)
    ks = jax.random.split(key, 24)

    def nrm(k, shape, s):
        return jax.random.normal(k, shape, F32) * s

    D = D_MODEL
    return {
        'x': nrm(ks[0], (BATCH, SEQ, D), 1.0),
        'c': nrm(ks[1], (BATCH, D), 1.0),
        'ctx': nrm(ks[2], (BATCH, CTX_LEN, D), 1.0),
        'c_ctx': nrm(ks[3], (D,), 1.0),
        'w_mod': nrm(ks[4], (DEPTH, D, 6 * D), 0.5 * D ** -0.5),
        'b_mod': nrm(ks[5], (DEPTH, 6 * D), 0.02),
        'norm1_g': 1.0 + nrm(ks[6], (DEPTH, D), 0.05),
        'norm2_g': 1.0 + nrm(ks[7], (DEPTH, D), 0.05),
        'w_in': nrm(ks[8], (DEPTH, D, IN_WIDTH), D ** -0.5),
        'win_qnorm_g': 1.0 + nrm(ks[9], (DEPTH, WIN_HEAD_DIM), 0.05),
        'win_knorm_g': 1.0 + nrm(ks[10], (DEPTH, WIN_HEAD_DIM), 0.05),
        'win_sink': nrm(ks[11], (DEPTH, WIN_HEADS), 0.5),
        'hg_lower': nrm(ks[12], (DEPTH, 2, HG_HEADS * HG_KEY_DIM), 1.0),
        'hg_onorm_g': 1.0 + nrm(ks[13], (DEPTH, HG_VAL_DIM), 0.05),
        'diff_qnorm_g': 1.0 + nrm(ks[14], (DEPTH, DIFF_QK_DIM), 0.05),
        'diff_knorm_g': 1.0 + nrm(ks[15], (DEPTH, DIFF_QK_DIM), 0.05),
        'diff_lambda': nrm(ks[16], (DEPTH, 4, DIFF_QK_DIM), 0.1),
        'diff_onorm_g': 1.0 + nrm(ks[17], (DEPTH, DIFF_V_DIM), 0.05),
        'w_out': nrm(ks[18], (DEPTH, MIX_WIDTH, D), MIX_WIDTH ** -0.5),
        'w_up': nrm(ks[19], (DEPTH, D, 2 * D_FF), D ** -0.5),
        'conv_w': nrm(ks[20], (DEPTH, CONV_WIDTH, 2 * D_FF), CONV_WIDTH ** -0.5),
        'conv_b': nrm(ks[21], (DEPTH, 2 * D_FF), 0.02),
        'w_down': nrm(ks[22], (DEPTH, D_FF, D), D_FF ** -0.5),
    }


def reference(x, c, ctx, c_ctx, w_mod, b_mod, norm1_g, norm2_g, w_in, win_qnorm_g, win_knorm_g, win_sink,
              hg_lower, hg_onorm_g, diff_qnorm_g, diff_knorm_g, diff_lambda, diff_onorm_g, w_out,
              w_up, conv_w, conv_b, w_down):
    L = x.shape[1]
    rope_a = axial_rope_tables(L, WIN_HEAD_DIM)
    rope_d = axial_rope_tables(L, DIFF_QK_DIM)
    lower = hgrn2_lower_bounds(hg_lower)
    xl, xc = x, ctx
    for l in range(DEPTH):
        need_ctx = l < DEPTH - 1
        lam_init = 0.8 - 0.6 * math.exp(-0.3 * l)
        sh1, sc1, g1, sh2, sc2, g2 = adaln(c, w_mod[l], b_mod[l])
        csh1, csc1, cg1, csh2, csc2, cg2 = adaln(c_ctx, w_mod[l], b_mod[l])
        hl = rms_norm(xl, norm1_g[l]) * (1.0 + sc1) + sh1
        hc = rms_norm(xc, norm1_g[l]) * (1.0 + csc1) + csh1
        ol, oc = token_mixers(hl, hc, w_in[l], win_qnorm_g[l], win_knorm_g[l], win_sink[l], lower[l],
                              hg_onorm_g[l], diff_qnorm_g[l], diff_knorm_g[l], diff_lambda[l], diff_onorm_g[l],
                              lam_init, rope_a, rope_d, need_ctx)
        xl = xl + g1 * (ol @ w_out[l])
        hl = rms_norm(xl, norm2_g[l]) * (1.0 + sc2) + sh2
        xl = xl + g2 * conv_ffn(hl, w_up[l], conv_w[l], conv_b[l], w_down[l])
        if need_ctx:
            xc = xc + cg1 * (oc @ w_out[l])
            hc = rms_norm(xc, norm2_g[l]) * (1.0 + csc2) + csh2
            xc = xc + cg2 * conv_ffn(hc, w_up[l], conv_w[l], conv_b[l], w_down[l])
    return xl
```

```python
import functools
import math

import numpy as np
import jax
import jax.numpy as jnp
from jax import lax
from jax.experimental import pallas as pl
from jax.experimental.pallas import tpu as pltpu

F32 = jnp.float32
BF16 = jnp.bfloat16

D_MODEL = 1024
GRID_W = 64
WIN_HEADS, WIN_KV_HEADS, WIN_HEAD_DIM = 8, 2, 64
WINDOW = 128
HG_HEADS, HG_KEY_DIM, HG_VAL_DIM = 4, 64, 64
DIFF_HEADS, DIFF_QK_DIM, DIFF_V_DIM = 4, 32, 64
D_FF = 11 * D_MODEL // 4
ROPE_BASE = 10000.0
EPS = 1e-6
LOG2E = math.log2(math.e)

_SPLITS = (512, 128, 128, 256, 256, 256, 256, 256, 256, 256, 256, 256)
_OFF = np.concatenate([[0], np.cumsum(_SPLITS)]).tolist()
IN_WIDTH = _OFF[-1]
(C_AQ, C_AK, C_AV, C_BQF, C_BFF, C_BQB, C_BFB, C_BI, C_BG, C_CQ, C_CK, C_CV) = _OFF[:-1]

LANES = 128
SUBLANES = 8
BF16_ROWS = 16
VMEM_LIMIT = 48 * 1024 * 1024

TOK_TILE = 256
ATT_BLOCK = 128
DIFF_TQ = 128
HG_BLOCK = 128
HG_SUB = 16
FF_CHUNK = 256
NEG = -1e30


def _cparams(*sem):
    return pltpu.CompilerParams(dimension_semantics=sem, vmem_limit_bytes=VMEM_LIMIT)


def _sigmoid(x):
    return 1.0 / (1.0 + jnp.exp(-x))


def _split3(x):
    hi = x.astype(BF16)
    r = x - hi.astype(F32)
    mid = r.astype(BF16)
    lo = (r - mid.astype(F32)).astype(BF16)
    return hi, mid, lo


def _group_sumsq(x, g):
    sq = x * x
    hi = sq.astype(BF16)
    lo = (sq - hi.astype(F32)).astype(BF16)
    return jnp.dot(hi, g, preferred_element_type=F32) + jnp.dot(lo, g, preferred_element_type=F32)


def _block_diag_ones(width, group):
    i = np.arange(width)
    return jnp.asarray((i[:, None] // group) == (i[None, :] // group), dtype=BF16)


def _prep_kernel(hl_ref, dl_ref, lb_ref, lam_ref, *, depth):
    raw = [hl_ref[l] for l in range(depth)]
    mx = raw[0]
    for l in range(1, depth):
        mx = jnp.maximum(mx, raw[l])
    ex = [jnp.exp(r - mx) for r in raw]
    tot = ex[0]
    for l in range(1, depth):
        tot = tot + ex[l]
    p = [e / tot for e in ex]
    run = p[0]
    lb_ref[0] = run - p[0]
    for l in range(1, depth):
        run = run + p[l]
        lb_ref[l] = run - p[0]
    for l in range(depth):
        d = dl_ref[l]
        a = jnp.sum(d[0:1] * d[1:2], axis=-1, keepdims=True)
        c = jnp.sum(d[2:3] * d[3:4], axis=-1, keepdims=True)
        lam_ref[l] = jnp.broadcast_to(jnp.exp(a) - jnp.exp(c), (1, LANES))


def _prep_params(hg_lower, diff_lambda):
    depth = hg_lower.shape[0]
    return pl.pallas_call(
        functools.partial(_prep_kernel, depth=depth),
        out_shape=(jax.ShapeDtypeStruct(hg_lower.shape, F32),
                   jax.ShapeDtypeStruct((depth, 1, LANES), F32)),
        name="prep_params",
    )(hg_lower.astype(F32), diff_lambda.astype(F32))


def _adaln_kernel(v_ref, w_ref, b_ref, o_ref):
    v = v_ref[...]
    a = v * _sigmoid(v)
    a_hi, a_mid, _ = _split3(a)
    w = w_ref[0]
    w_hi, w_mid, _ = _split3(w)
    acc = jnp.dot(a_hi, w_hi, preferred_element_type=F32)
    acc += jnp.dot(a_mid, w_hi, preferred_element_type=F32)
    acc += jnp.dot(a_hi, w_mid, preferred_element_type=F32)
    o_ref[0] = acc + b_ref[0]


def _adaln(vecs, w_mod, b_mod):
    depth, d, n6 = w_mod.shape
    rows = vecs.shape[0]
    cb = 768
    assert n6 % cb == 0
    return pl.pallas_call(
        _adaln_kernel,
        grid=(depth, n6 // cb),
        in_specs=[pl.BlockSpec((rows, d), lambda l, j: (0, 0)),
                  pl.BlockSpec((1, d, cb), lambda l, j: (l, 0, j)),
                  pl.BlockSpec((1, 1, cb), lambda l, j: (l, 0, j))],
        out_specs=pl.BlockSpec((1, rows, cb), lambda l, j: (l, 0, j)),
        out_shape=jax.ShapeDtypeStruct((depth, rows, n6), F32),
        compiler_params=_cparams("arbitrary", "arbitrary"),
        name="adaln",
    )(vecs, w_mod, b_mod.reshape(depth, 1, n6))


def _rope_tables(L, Lc, dim):
    rows = L // GRID_W
    row = jnp.repeat(jnp.arange(rows, dtype=F32), GRID_W)
    col = jnp.tile(jnp.arange(GRID_W, dtype=F32), rows)
    axis_dim = dim // 2
    n = axis_dim // 2
    inv = jnp.power(ROPE_BASE, -jnp.arange(n, dtype=F32) * 2.0 / axis_dim)
    ar = row[:, None] * inv[None, :]
    ac = col[:, None] * inv[None, :]
    cos = jnp.concatenate([jnp.cos(ar), jnp.cos(ar), jnp.cos(ac), jnp.cos(ac)], axis=-1)
    sin = jnp.concatenate([-jnp.sin(ar), jnp.sin(ar), -jnp.sin(ac), jnp.sin(ac)], axis=-1)
    cos = jnp.concatenate([jnp.ones((Lc, dim), F32), cos], axis=0)
    sin = jnp.concatenate([jnp.zeros((Lc, dim), F32), sin], axis=0)
    rep = LANES // dim
    return jnp.tile(cos, (1, rep)), jnp.tile(sin, (1, rep))


def _rope(x, cos, sin, quarter):
    lane = lax.broadcasted_iota(jnp.int32, x.shape, 1)
    first = (lane % (2 * quarter)) < quarter
    partner = jnp.where(first, pltpu.roll(x, LANES - quarter, 1), pltpu.roll(x, quarter, 1))
    return x * cos + partner * sin


def _inproj_kernel(x_ref, mod_ref, n1g_ref, w_ref, cosa_ref, sina_ref, cosd_ref, sind_ref,
                   gqa_ref, gka_ref, gqd_ref, gkd_ref, lb_ref, g64_ref, g32_ref,
                   qa_ref, ka_ref, va_ref, hqf_ref, hlff_ref, hkf_ref, hqb_ref, hlfb_ref, hkb_ref,
                   hi_ref, hg_ref, qd_ref, kd_ref, vd_ref, *, n_batch, n_ctx_tiles):
    b = pl.program_id(0)
    t = pl.program_id(1)
    mrow = jnp.where(t < n_ctx_tiles, n_batch, b)
    m = mod_ref[pl.ds(mrow, 1), :]
    shift, scale = m[:, 0:D_MODEL], m[:, D_MODEL:2 * D_MODEL]

    x = x_ref[0]
    ms = jnp.mean(x * x, axis=-1, keepdims=True)
    h = (x * lax.rsqrt(ms + EPS) * n1g_ref[...]) * (1.0 + scale) + shift
    h = h.astype(BF16)

    def proj(c0, width):
        return jnp.dot(h, w_ref[:, c0:c0 + width], preferred_element_type=F32)

    g64 = g64_ref[...]
    g32 = g32_ref[...]
    cosa, sina = cosa_ref[...], sina_ref[...]
    cosd, sind = cosd_ref[...], sind_ref[...]

    for c in range(2):
        p = proj(C_AQ + 256 * c, 256)
        qn = p * lax.rsqrt(_group_sumsq(p, g64) * (1.0 / WIN_HEAD_DIM) + EPS)
        qn = qn * (gqa_ref[:, 256 * c:256 * (c + 1)] * (WIN_HEAD_DIM ** -0.5 * LOG2E))
        for j in range(2):
            r = _rope(qn[:, LANES * j:LANES * (j + 1)], cosa, sina, WIN_HEAD_DIM // 4)
            qa_ref[0, :, 256 * c + LANES * j:256 * c + LANES * (j + 1)] = r.astype(BF16)
    p = proj(C_AK, 128)
    kn = p * lax.rsqrt(_group_sumsq(p, g64[:LANES, :LANES]) * (1.0 / WIN_HEAD_DIM) + EPS) * gka_ref[...]
    ka_ref[0] = _rope(kn, cosa, sina, WIN_HEAD_DIM // 4).astype(BF16)
    va_ref[0] = proj(C_AV, 128).astype(BF16)

    for d, (cq, cf, q_ref, lf_ref, k_ref) in enumerate(((C_BQF, C_BFF, hqf_ref, hlff_ref, hkf_ref),
                                                        (C_BQB, C_BFB, hqb_ref, hlfb_ref, hkb_ref))):
        q_ref[0] = proj(cq, 256)
        lb = lb_ref[d:d + 1, :]
        f = lb + (1.0 - lb) * _sigmoid(proj(cf, 256))
        lf_ref[0] = jnp.log(f)
        k_ref[0] = 1.0 - f
    hi_ref[0] = proj(C_BI, 256).astype(BF16)
    g = proj(C_BG, 256)
    hg_ref[0] = g * _sigmoid(g)

    for c0, gain_ref, o_ref, mult in ((C_CQ, gqd_ref, qd_ref, DIFF_QK_DIM ** -0.5 * LOG2E),
                                      (C_CK, gkd_ref, kd_ref, 1.0)):
        p = proj(c0, 256)
        pn = p * lax.rsqrt(_group_sumsq(p, g32) * (1.0 / DIFF_QK_DIM) + EPS) * (gain_ref[...] * mult)
        for j in range(2):
            r = _rope(pn[:, LANES * j:LANES * (j + 1)], cosd, sind, DIFF_QK_DIM // 4)
            o_ref[0, :, LANES * j:LANES * (j + 1)] = r.astype(BF16)
    vd_ref[0] = proj(C_CV, 256).astype(BF16)


def _inproj(xc, mod_l, n1g, w_in, tabs, gains, lb_l, g64, g32, n_batch, Lc):
    B, Lt, D = xc.shape
    nt = Lt // TOK_TILE
    cosa, sina, cosd, sind = tabs
    gqa, gka, gqd, gkd = gains
    full = lambda a: pl.BlockSpec(a.shape, lambda b, t: (0,) * a.ndim)
    tab = pl.BlockSpec((TOK_TILE, LANES), lambda b, t: (t, 0))
    tok = lambda w: pl.BlockSpec((1, TOK_TILE, w), lambda b, t: (b, t, 0))
    widths = (512, 128, 128, 256, 256, 256, 256, 256, 256, 256, 256, 256, 256, 256)
    dtypes = (BF16, BF16, BF16, F32, F32, F32, F32, F32, F32, BF16, F32, BF16, BF16, BF16)
    return pl.pallas_call(
        functools.partial(_inproj_kernel, n_batch=n_batch, n_ctx_tiles=Lc // TOK_TILE),
        grid=(B, nt),
        in_specs=[tok(D), full(mod_l), full(n1g), full(w_in), tab, tab, tab, tab,
                  full(gqa), full(gka), full(gqd), full(gkd), full(lb_l), full(g64), full(g32)],
        out_specs=[tok(w) for w in widths],
        out_shape=[jax.ShapeDtypeStruct((B, Lt, w), dt) for w, dt in zip(widths, dtypes)],
        compiler_params=_cparams("parallel", "arbitrary"),
        name="inproj",
    )(xc, mod_l, n1g, w_in, cosa, sina, cosd, sind, gqa, gka, gqd, gkd, lb_l, g64, g32)


def _softmax_sink_pv(s_list, v_list, sink):
    m = jnp.full((s_list[0].shape[0], 1), sink, F32)
    for s in s_list:
        m = jnp.maximum(m, jnp.max(s, axis=-1, keepdims=True))
    den = jnp.exp2(sink - m)
    acc = None
    for s, v in zip(s_list, v_list):
        e = jnp.exp2(s - m)
        den = den + jnp.sum(e, axis=-1, keepdims=True)
        pv = jnp.dot(e.astype(BF16), v, preferred_element_type=F32)
        acc = pv if acc is None else acc + pv
    return acc / den


def _win_attn_kernel(sink_ref, q_ref, kp_ref, kc_ref, kn_ref, kx_ref, vp_ref, vc_ref, vn_ref, vx_ref, o_ref):
    j = pl.program_id(1)
    nb = pl.num_programs(1)
    t = lax.broadcasted_iota(jnp.int32, (ATT_BLOCK, ATT_BLOCK), 0)
    s = lax.broadcasted_iota(jnp.int32, (ATT_BLOCK, ATT_BLOCK), 1)
    mask_prev = (s >= t) & (j > 0)
    mask_next = (s <= t) & (j < nb - 1)
    group = WIN_HEADS // WIN_KV_HEADS
    for h in range(WIN_HEADS):
        hk = h // group
        q = q_ref[0, h]
        s_prev = jnp.where(mask_prev, jnp.dot(q, kp_ref[0, hk], preferred_element_type=F32), NEG)
        s_cur = jnp.dot(q, kc_ref[0, hk], preferred_element_type=F32)
        s_next = jnp.where(mask_next, jnp.dot(q, kn_ref[0, hk], preferred_element_type=F32), NEG)
        s_ctx = jnp.dot(q, kx_ref[0, hk], preferred_element_type=F32)
        o = _softmax_sink_pv([s_prev, s_cur, s_next, s_ctx],
                             [vp_ref[0, hk], vc_ref[0, hk], vn_ref[0, hk], vx_ref[0, hk]], sink_ref[h])
        o_ref[0, h] = o.astype(BF16)


def _ctx_attn_kernel(sink_ref, q_ref, kx_ref, vx_ref, o_ref):
    group = WIN_HEADS // WIN_KV_HEADS
    for h in range(WIN_HEADS):
        hk = h // group
        s_ctx = jnp.dot(q_ref[0, h], kx_ref[0, hk], preferred_element_type=F32)
        o_ref[0, h] = _softmax_sink_pv([s_ctx], [vx_ref[0, hk]], sink_ref[h]).astype(BF16)


def _win_attention(sink, q_h, kT, v_h, L, Lc, need_ctx):
    B = q_h.shape[0]
    nb = L // ATT_BLOCK
    off = Lc // ATT_BLOCK
    dh = WIN_HEAD_DIM
    smem = pl.BlockSpec(memory_space=pltpu.SMEM)
    kblk = lambda f: pl.BlockSpec((1, WIN_KV_HEADS, dh, ATT_BLOCK), lambda b, j: (b, 0, 0, off + f(j)))
    vblk = lambda f: pl.BlockSpec((1, WIN_KV_HEADS, ATT_BLOCK, dh), lambda b, j: (b, 0, off + f(j), 0))
    prev = lambda j: jnp.maximum(j - 1, 0)
    cur = lambda j: j
    nxt = lambda j: jnp.minimum(j + 1, nb - 1)
    kctx = pl.BlockSpec((1, WIN_KV_HEADS, dh, Lc), lambda b, j: (b, 0, 0, 0))
    vctx = pl.BlockSpec((1, WIN_KV_HEADS, Lc, dh), lambda b, j: (b, 0, 0, 0))
    o_lat = pl.pallas_call(
        _win_attn_kernel,
        grid=(B, nb),
        in_specs=[smem, pl.BlockSpec((1, WIN_HEADS, ATT_BLOCK, dh), lambda b, j: (b, 0, off + j, 0)),
                  kblk(prev), kblk(cur), kblk(nxt), kctx, vblk(prev), vblk(cur), vblk(nxt), vctx],
        out_specs=pl.BlockSpec((1, WIN_HEADS, ATT_BLOCK, dh), lambda b, j: (b, 0, j, 0)),
        out_shape=jax.ShapeDtypeStruct((B, WIN_HEADS, L, dh), BF16),
        compiler_params=_cparams("parallel", "arbitrary"),
        name="win_attn",
    )(sink, q_h, kT, kT, kT, kT, v_h, v_h, v_h, v_h)
    if not need_ctx:
        return o_lat, None
    o_ctx = pl.pallas_call(
        _ctx_attn_kernel,
        grid=(B,),
        in_specs=[smem, pl.BlockSpec((1, WIN_HEADS, Lc, dh), lambda b: (b, 0, 0, 0)),
                  pl.BlockSpec((1, WIN_KV_HEADS, dh, Lc), lambda b: (b, 0, 0, 0)),
                  pl.BlockSpec((1, WIN_KV_HEADS, Lc, dh), lambda b: (b, 0, 0, 0))],
        out_specs=pl.BlockSpec((1, WIN_HEADS, Lc, dh), lambda b: (b, 0, 0, 0)),
        out_shape=jax.ShapeDtypeStruct((B, WIN_HEADS, Lc, dh), BF16),
        compiler_params=_cparams("parallel"),
        name="ctx_attn",
    )(sink, q_h, kT, v_h)
    return o_lat, o_ctx


def _hgrn_kernel(q_ref, lf_ref, k_ref, v_ref, g64_ref, o_ref, st_ref, *, reverse):
    @pl.when(pl.program_id(1) == 0)
    def _():
        st_ref[...] = jnp.zeros_like(st_ref)

    width = HG_HEADS * HG_KEY_DIM
    ti = lax.broadcasted_iota(jnp.int32, (HG_SUB, HG_SUB), 0)
    si = lax.broadcasted_iota(jnp.int32, (HG_SUB, HG_SUB), 1)
    tri = ((si >= ti) if reverse else (si <= ti)).astype(BF16)
    rows = lax.broadcasted_iota(jnp.int32, (HG_SUB, width), 0)
    bi = lax.broadcasted_iota(jnp.int32, (width, width), 0) // HG_VAL_DIM
    bj = lax.broadcasted_iota(jnp.int32, (width, width), 1) // HG_KEY_DIM
    same_head = bi == bj
    g64 = g64_ref[...]
    n_sub = HG_BLOCK // HG_SUB
    end_row = 0 if reverse else HG_SUB - 1

    def body(i, carry):
        c = (n_sub - 1 - i) if reverse else i
        r0 = pl.multiple_of(c * HG_SUB, HG_SUB)
        lf = lf_ref[0, pl.ds(r0, HG_SUB), :]
        q = q_ref[0, pl.ds(r0, HG_SUB), :]
        k = k_ref[0, pl.ds(r0, HG_SUB), :]
        v = v_ref[0, pl.ds(r0, HG_SUB), :]
        vf = v.astype(F32)

        hi, mid, lo = _split3(lf)
        bcum = (jnp.dot(tri, hi, preferred_element_type=F32) + jnp.dot(tri, mid, preferred_element_type=F32)
                + jnp.dot(tri, lo, preferred_element_type=F32))
        b_end = bcum[end_row:end_row + 1, :]

        st = st_ref[...]
        st_b = jnp.where(same_head, st, 0.0).astype(BF16)
        qt = (q * jnp.exp(bcum)).astype(BF16)
        o = lax.dot_general(qt, st_b, (((1,), (1,)), ((), ())), preferred_element_type=F32)

        khat = (k * jnp.exp(b_end - bcum)).astype(BF16)
        upd = lax.dot_general(v, khat, (((0,), (0,)), ((), ())), preferred_element_type=F32)
        st_ref[...] = st * jnp.exp(b_end) + upd

        ws = []
        for s in range(HG_SUB):
            valid = (rows <= s) if reverse else (rows >= s)
            rel = jnp.minimum(bcum - bcum[s:s + 1, :], 0.0)
            ws.append(jnp.where(valid, q * jnp.exp(rel) * k[s:s + 1, :], 0.0).astype(BF16))
        a = jnp.dot(jnp.concatenate(ws, axis=0), g64, preferred_element_type=F32)
        for s in range(HG_SUB):
            o = o + a[s * HG_SUB:(s + 1) * HG_SUB, :] * vf[s:s + 1, :]
        o_ref[0, pl.ds(r0, HG_SUB), :] = o
        return carry

    lax.fori_loop(0, n_sub, body, 0)


def _hgrn_scan(q, lf, k, v, g64, Lc, reverse):
    B, Lt, W = q.shape
    nblk = Lt // HG_BLOCK
    nctx = Lc // HG_BLOCK
    if reverse:
        order = lambda i: jnp.where(i < nctx, nctx - 1 - i, nctx + (nblk - 1 - i))
    else:
        order = lambda i: i
    spec = pl.BlockSpec((1, HG_BLOCK, W), lambda b, i: (b, order(i), 0))
    return pl.pallas_call(
        functools.partial(_hgrn_kernel, reverse=reverse),
        grid=(B, nblk),
        in_specs=[spec, spec, spec, spec, pl.BlockSpec(g64.shape, lambda b, i: (0, 0))],
        out_specs=spec,
        out_shape=jax.ShapeDtypeStruct((B, Lt, W), F32),
        scratch_shapes=[pltpu.VMEM((W, W), F32)],
        compiler_params=_cparams("parallel", "arbitrary"),
        name="hgrn_bwd" if reverse else "hgrn_fwd",
    )(q, lf, k, v, g64)


def _diff_attn_kernel(lam_ref, q_ref, kT_ref, v_ref, og_ref, o_ref, *, key_block, n_keys, post_scale):
    nkb = n_keys // key_block
    outs = []
    for c in range(2):
        q = q_ref[0, c]

        def body(kb, carry):
            m, acc = carry
            k0 = pl.multiple_of(kb * key_block, key_block)
            s = jnp.dot(q, kT_ref[0, c, :, pl.ds(k0, key_block)], preferred_element_type=F32)
            m_new = jnp.maximum(m, jnp.max(s, axis=-1, keepdims=True))
            e = jnp.exp2(s - m_new).astype(BF16)
            pv = jnp.dot(e, v_ref[0, 0, pl.ds(k0, key_block), :], preferred_element_type=F32)
            return m_new, acc * jnp.exp2(m - m_new) + pv

        m0 = jnp.full((DIFF_TQ, 1), NEG, F32)
        acc0 = jnp.zeros((DIFF_TQ, LANES), F32)
        _, acc = lax.fori_loop(0, nkb, body, (m0, acc0))
        outs.append(acc[:, :DIFF_V_DIM] / acc[:, DIFF_V_DIM:DIFF_V_DIM + 1])
    o = outs[0] - lam_ref[0] * outs[1]
    ms = jnp.mean(o * o, axis=-1, keepdims=True)
    o_ref[0, 0] = (o * lax.rsqrt(ms + EPS) * og_ref[...] * post_scale).astype(BF16)


def _pick_key_block(n):
    best = LANES
    for kb in range(LANES, min(n, 1024) + 1, LANES):
        if n % kb == 0:
            best = kb
    return best


def _diff_attention(lam, q_h, kT, v_aug, og, q_off, n_q, n_keys, post_scale):
    B = q_h.shape[0]
    dq, dv = DIFF_QK_DIM, DIFF_V_DIM
    nq = n_q // DIFF_TQ
    qo = q_off // DIFF_TQ
    return pl.pallas_call(
        functools.partial(_diff_attn_kernel, key_block=_pick_key_block(n_keys), n_keys=n_keys,
                          post_scale=post_scale),
        grid=(B, DIFF_HEADS, nq),
        in_specs=[pl.BlockSpec(memory_space=pltpu.SMEM),
                  pl.BlockSpec((1, 2, DIFF_TQ, dq), lambda b, h, i: (b, h, qo + i, 0)),
                  pl.BlockSpec((1, 2, dq, n_keys), lambda b, h, i: (b, h, 0, 0)),
                  pl.BlockSpec((1, 1, n_keys, LANES), lambda b, h, i: (b, h, 0, 0)),
                  pl.BlockSpec((1, dv), lambda b, h, i: (0, 0))],
        out_specs=pl.BlockSpec((1, 1, DIFF_TQ, dv), lambda b, h, i: (b, h, i, 0)),
        out_shape=jax.ShapeDtypeStruct((B, DIFF_HEADS, n_q, dv), BF16),
        compiler_params=_cparams("parallel", "parallel", "arbitrary"),
        name="diff_attn",
    )(lam, q_h, kT, v_aug, og)


def _outproj_kernel(x_ref, oa_ref, of_ref, ob_ref, gate_ref, od_ref, mod_ref, w_ref, hog_ref, n2g_ref, g64_ref,
                    xm_ref, h2_ref, *, n_batch, n_ctx_tiles, tile_off):
    b = pl.program_id(0)
    t = pl.program_id(1) + tile_off
    mrow = jnp.where(t < n_ctx_tiles, n_batch, b)
    m = mod_ref[pl.ds(mrow, 1), :]
    gate1 = m[:, 2 * D_MODEL:3 * D_MODEL]
    shift2, scale2 = m[:, 3 * D_MODEL:4 * D_MODEL], m[:, 4 * D_MODEL:5 * D_MODEL]

    o = of_ref[0] + ob_ref[0]
    on = o * lax.rsqrt(_group_sumsq(o, g64_ref[...]) * (1.0 / HG_VAL_DIM) + EPS) * hog_ref[...]
    ob = (on * gate_ref[0]).astype(BF16)

    wa, wb = WIN_HEADS * WIN_HEAD_DIM, HG_HEADS * HG_VAL_DIM
    mix = jnp.dot(oa_ref[0], w_ref[0:wa, :], preferred_element_type=F32)
    mix += jnp.dot(ob, w_ref[wa:wa + wb, :], preferred_element_type=F32)
    mix += jnp.dot(od_ref[0], w_ref[wa + wb:, :], preferred_element_type=F32)

    x = x_ref[0] + gate1 * mix
    xm_ref[0] = x
    ms = jnp.mean(x * x, axis=-1, keepdims=True)
    h = (x * lax.rsqrt(ms + EPS) * n2g_ref[...]) * (1.0 + scale2) + shift2
    h2_ref[0] = h.astype(BF16)


def _outproj(xc, oa, of, ob, gate, od, mod_l, w_out, hog, n2g, g64, n_batch, Lc, tile_off, mix_off):
    B, Lt, D = xc.shape
    nt = Lt // TOK_TILE - tile_off
    full = lambda a: pl.BlockSpec(a.shape, lambda b, t: (0,) * a.ndim)
    comb = lambda w: pl.BlockSpec((1, TOK_TILE, w), lambda b, t: (b, t + tile_off, 0))
    part = lambda w: pl.BlockSpec((1, TOK_TILE, w), lambda b, t: (b, t + tile_off - mix_off, 0))
    outs = pl.BlockSpec((1, TOK_TILE, D), lambda b, t: (b, t, 0))
    return pl.pallas_call(
        functools.partial(_outproj_kernel, n_batch=n_batch, n_ctx_tiles=Lc // TOK_TILE, tile_off=tile_off),
        grid=(B, nt),
        in_specs=[comb(D), part(oa.shape[-1]), comb(256), comb(256), comb(256), part(od.shape[-1]),
                  full(mod_l), full(w_out), full(hog), full(n2g), full(g64)],
        out_specs=[outs, outs],
        out_shape=[jax.ShapeDtypeStruct((B, nt * TOK_TILE, D), F32),
                   jax.ShapeDtypeStruct((B, nt * TOK_TILE, D), BF16)],
        compiler_params=_cparams("parallel", "arbitrary"),
        name="outproj",
    )(xc, oa, of, ob, gate, od, mod_l, w_out, hog, n2g, g64)


def _ffn_kernel(x_ref, h_ref, hp_ref, hn_ref, mod_ref, wu_ref, cw_ref, cb_ref, wd_ref, o_ref,
                *, n_batch, n_ctx_tiles, tile_off, n_tiles_total):
    b = pl.program_id(0)
    t = pl.program_id(1) + tile_off
    mrow = jnp.where(t < n_ctx_tiles, n_batch, b)
    gate2 = mod_ref[pl.ds(mrow, 1), :][:, 5 * D_MODEL:6 * D_MODEL]

    prev_ok = jnp.logical_and(t != 0, t != n_ctx_tiles)
    next_ok = jnp.logical_and(t != n_ctx_tiles - 1, t != n_tiles_total - 1)
    hp = jnp.where(prev_ok, hp_ref[0], jnp.zeros_like(hp_ref[0]))
    hn = jnp.where(next_ok, hn_ref[0], jnp.zeros_like(hn_ref[0]))
    hcat = jnp.concatenate([hp, h_ref[0], hn], axis=0)

    lo = BF16_ROWS
    acc = jnp.zeros((TOK_TILE, D_MODEL), F32)
    for c in range(D_FF // FF_CHUNK):
        ys = []
        for base in (0, D_FF):
            c0 = base + c * FF_CHUNK
            u = jnp.dot(hcat, wu_ref[:, c0:c0 + FF_CHUNK], preferred_element_type=F32)
            w = cw_ref[:, c0:c0 + FF_CHUNK]
            y = (cb_ref[:, c0:c0 + FF_CHUNK] + u[lo - 1:lo - 1 + TOK_TILE] * w[0:1]
                 + u[lo:lo + TOK_TILE] * w[1:2] + u[lo + 1:lo + 1 + TOK_TILE] * w[2:3])
            ys.append(y)
        act = (ys[0] * _sigmoid(ys[0]) * ys[1]).astype(BF16)
        acc += jnp.dot(act, wd_ref[c * FF_CHUNK:(c + 1) * FF_CHUNK, :], preferred_element_type=F32)
    o_ref[0] = x_ref[0] + gate2 * acc


def _conv_ffn(xm, h2, mod_l, w_up, conv_w, conv_b, w_down, n_batch, Lc, tile_off, n_tiles_total):
    B, T, D = xm.shape
    nt = T // TOK_TILE
    r = TOK_TILE // BF16_ROWS
    last = T // BF16_ROWS - 1
    full = lambda a: pl.BlockSpec(a.shape, lambda b, t: (0,) * a.ndim)
    tok = pl.BlockSpec((1, TOK_TILE, D), lambda b, t: (b, t, 0))
    halo_prev = pl.BlockSpec((1, BF16_ROWS, D), lambda b, t: (b, jnp.maximum(t * r - 1, 0), 0))
    halo_next = pl.BlockSpec((1, BF16_ROWS, D), lambda b, t: (b, jnp.minimum((t + 1) * r, last), 0))
    return pl.pallas_call(
        functools.partial(_ffn_kernel, n_batch=n_batch, n_ctx_tiles=Lc // TOK_TILE, tile_off=tile_off,
                          n_tiles_total=n_tiles_total),
        grid=(B, nt),
        in_specs=[tok, tok, halo_prev, halo_next, full(mod_l), full(w_up), full(conv_w), full(conv_b),
                  full(w_down)],
        out_specs=tok,
        out_shape=jax.ShapeDtypeStruct((B, T, D), F32),
        compiler_params=_cparams("parallel", "arbitrary"),
        name="conv_ffn",
    )(xm, h2, h2, h2, mod_l, w_up, conv_w, conv_b, w_down)


def kernel(x, c, ctx, c_ctx, w_mod, b_mod, norm1_g, norm2_g, w_in, win_qnorm_g, win_knorm_g, win_sink,
           hg_lower, hg_onorm_g, diff_qnorm_g, diff_knorm_g, diff_lambda, diff_onorm_g, w_out,
           w_up, conv_w, conv_b, w_down):
    B, L, D = x.shape
    Lc = ctx.shape[1]
    depth = w_mod.shape[0]
    Lt = Lc + L
    assert D == D_MODEL and L % GRID_W == 0
    assert Lc % TOK_TILE == 0 and L % TOK_TILE == 0 and TOK_TILE % HG_BLOCK == 0 and TOK_TILE % ATT_BLOCK == 0
    assert B + 1 <= SUBLANES

    vecs = jnp.zeros((SUBLANES, D), F32).at[:B].set(c.astype(F32)).at[B].set(c_ctx.astype(F32))
    mod = _adaln(vecs, w_mod.astype(F32), b_mod.astype(F32))
    lower, lam_raw = _prep_params(hg_lower, diff_lambda)

    tabs = _rope_tables(L, Lc, WIN_HEAD_DIM) + _rope_tables(L, Lc, DIFF_QK_DIM)
    g64 = _block_diag_ones(256, 64)
    g32 = _block_diag_ones(256, 32)
    row = lambda v, reps: jnp.tile(v.astype(F32), reps).reshape(1, -1)

    xc = jnp.concatenate([ctx.astype(F32), x.astype(F32)], axis=1)
    n_tiles_total = Lt // TOK_TILE
    nct = Lc // TOK_TILE

    for l in range(depth):
        need_ctx = l < depth - 1
        lam_init = 0.8 - 0.6 * math.exp(-0.3 * l)
        gains = (row(win_qnorm_g[l], WIN_HEADS), row(win_knorm_g[l], WIN_KV_HEADS),
                 row(diff_qnorm_g[l], 2 * DIFF_HEADS), row(diff_knorm_g[l], 2 * DIFF_HEADS))
        (qa, ka, va, hqf, hlff, hkf, hqb, hlfb, hkb, hi, hgate, qd, kd, vd) = _inproj(
            xc, mod[l], norm1_g[l].reshape(1, D).astype(F32), w_in[l].astype(BF16), tabs, gains, lower[l],
            g64, g32, B, Lc)

        qa_h = qa.reshape(B, Lt, WIN_HEADS, WIN_HEAD_DIM).transpose(0, 2, 1, 3)
        kaT = ka.reshape(B, Lt, WIN_KV_HEADS, WIN_HEAD_DIM).transpose(0, 2, 3, 1)
        va_h = va.reshape(B, Lt, WIN_KV_HEADS, WIN_HEAD_DIM).transpose(0, 2, 1, 3)
        sink = win_sink[l].astype(F32) * LOG2E
        oa_lat, oa_ctx = _win_attention(sink, qa_h, kaT, va_h, L, Lc, need_ctx)

        of = _hgrn_scan(hqf, hlff, hkf, hi, g64, Lc, reverse=False)
        ob = _hgrn_scan(hqb, hlfb, hkb, hi, g64, Lc, reverse=True)

        qd_h = qd.reshape(B, Lt, 2 * DIFF_HEADS, DIFF_QK_DIM).transpose(0, 2, 1, 3)
        kdT = kd.reshape(B, Lt, 2 * DIFF_HEADS, DIFF_QK_DIM).transpose(0, 2, 3, 1)
        vd_h = vd.reshape(B, Lt, DIFF_HEADS, DIFF_V_DIM).transpose(0, 2, 1, 3)
        pad = jnp.zeros((B, DIFF_HEADS, Lt, LANES - DIFF_V_DIM), BF16).at[..., 0].set(1.0)
        vd_aug = jnp.concatenate([vd_h, pad], axis=-1)
        lam = lam_raw[l, 0, :1] + lam_init
        dog = diff_onorm_g[l].reshape(1, DIFF_V_DIM).astype(F32)
        od_lat = _diff_attention(lam, qd_h, kdT, vd_aug, dog, Lc, L, Lt, 1.0 - lam_init)

        def token_major(o):
            return o.transpose(0, 2, 1, 3).reshape(B, o.shape[2], -1)

        if need_ctx:
            od_ctx = _diff_attention(lam, qd_h, kdT, vd_aug, dog, 0, Lc, Lc, 1.0 - lam_init)
            oa = jnp.concatenate([token_major(oa_ctx), token_major(oa_lat)], axis=1)
            od = jnp.concatenate([token_major(od_ctx), token_major(od_lat)], axis=1)
            tile_off, mix_off = 0, 0
        else:
            oa, od = token_major(oa_lat), token_major(od_lat)
            tile_off, mix_off = nct, nct

        xm, h2 = _outproj(xc, oa, of, ob, hgate, od, mod[l], w_out[l].astype(BF16),
                          row(hg_onorm_g[l], HG_HEADS), norm2_g[l].reshape(1, D).astype(F32), g64,
                          B, Lc, tile_off, mix_off)
        xc = _conv_ffn(xm, h2, mod[l], w_up[l].astype(BF16), conv_w[l].astype(F32),
                       conv_b[l].reshape(1, -1).astype(F32), w_down[l].astype(BF16),
                       B, Lc, tile_off, n_tiles_total)

    return xc
```

```python
import functools
import math

import numpy as np
import jax
import jax.numpy as jnp
from jax import lax
from jax.experimental import pallas as pl
from jax.experimental.pallas import tpu as pltpu

F32 = jnp.float32
BF16 = jnp.bfloat16

D_MODEL = 1024
GRID_W = 64
WIN_HEADS, WIN_KV_HEADS, WIN_HEAD_DIM = 8, 2, 64
WINDOW = 128
HG_HEADS, HG_KEY_DIM, HG_VAL_DIM = 4, 64, 64
DIFF_HEADS, DIFF_QK_DIM, DIFF_V_DIM = 4, 32, 64
D_FF = 11 * D_MODEL // 4
ROPE_BASE = 10000.0
EPS = 1e-6
LOG2E = math.log2(math.e)

_SPLITS = (512, 128, 128, 256, 256, 256, 256, 256, 256, 256, 256, 256)
_OFF = np.concatenate([[0], np.cumsum(_SPLITS)]).tolist()
IN_WIDTH = _OFF[-1]
(C_AQ, C_AK, C_AV, C_BQF, C_BFF, C_BQB, C_BFB, C_BI, C_BG, C_CQ, C_CK, C_CV) = _OFF[:-1]

LANES = 128
SUBLANES = 8
BF16_ROWS = 16
VMEM_LIMIT = 48 * 1024 * 1024

TOK_TILE = 256
ATT_BLOCK = 128
DIFF_TQ = 128
HG_BLOCK = 256
HG_GROUP = 128
HG_SUB = 16
FF_CHUNK = 256
NEG = -1e30


def _cparams(*sem):
    return pltpu.CompilerParams(dimension_semantics=sem, vmem_limit_bytes=VMEM_LIMIT)


def _sigmoid(x):
    return 1.0 / (1.0 + jnp.exp(-x))


def _split3(x):
    hi = x.astype(BF16)
    r = x - hi.astype(F32)
    mid = r.astype(BF16)
    lo = (r - mid.astype(F32)).astype(BF16)
    return hi, mid, lo


def _group_sumsq(x, g):
    sq = x * x
    hi = sq.astype(BF16)
    lo = (sq - hi.astype(F32)).astype(BF16)
    return jnp.dot(hi, g, preferred_element_type=F32) + jnp.dot(lo, g, preferred_element_type=F32)


def _block_diag_ones(width, group):
    i = np.arange(width)
    return jnp.asarray((i[:, None] // group) == (i[None, :] // group), dtype=BF16)


def _prep_kernel(hl_ref, dl_ref, lb_ref, lam_ref, *, depth):
    raw = [hl_ref[l] for l in range(depth)]
    mx = raw[0]
    for l in range(1, depth):
        mx = jnp.maximum(mx, raw[l])
    ex = [jnp.exp(r - mx) for r in raw]
    tot = ex[0]
    for l in range(1, depth):
        tot = tot + ex[l]
    p = [e / tot for e in ex]
    run = p[0]
    lb_ref[0] = run - p[0]
    for l in range(1, depth):
        run = run + p[l]
        lb_ref[l] = run - p[0]
    for l in range(depth):
        d = dl_ref[l]
        a = jnp.sum(d[0:1] * d[1:2], axis=-1, keepdims=True)
        c = jnp.sum(d[2:3] * d[3:4], axis=-1, keepdims=True)
        lam_ref[l] = jnp.broadcast_to(jnp.exp(a) - jnp.exp(c), (1, LANES))


def _prep_params(hg_lower, diff_lambda):
    depth = hg_lower.shape[0]
    return pl.pallas_call(
        functools.partial(_prep_kernel, depth=depth),
        out_shape=(jax.ShapeDtypeStruct(hg_lower.shape, F32),
                   jax.ShapeDtypeStruct((depth, 1, LANES), F32)),
        name="prep_params",
    )(hg_lower.astype(F32), diff_lambda.astype(F32))


def _adaln_kernel(v_ref, w_ref, b_ref, o_ref):
    v = v_ref[...]
    a = v * _sigmoid(v)
    a_hi, a_mid, _ = _split3(a)
    w = w_ref[0]
    w_hi, w_mid, _ = _split3(w)
    acc = jnp.dot(a_hi, w_hi, preferred_element_type=F32)
    acc += jnp.dot(a_mid, w_hi, preferred_element_type=F32)
    acc += jnp.dot(a_hi, w_mid, preferred_element_type=F32)
    o_ref[0] = acc + b_ref[0]


def _adaln(vecs, w_mod, b_mod):
    depth, d, n6 = w_mod.shape
    rows = vecs.shape[0]
    cb = 768
    assert n6 % cb == 0
    return pl.pallas_call(
        _adaln_kernel,
        grid=(depth, n6 // cb),
        in_specs=[pl.BlockSpec((rows, d), lambda l, j: (0, 0)),
                  pl.BlockSpec((1, d, cb), lambda l, j: (l, 0, j)),
                  pl.BlockSpec((1, 1, cb), lambda l, j: (l, 0, j))],
        out_specs=pl.BlockSpec((1, rows, cb), lambda l, j: (l, 0, j)),
        out_shape=jax.ShapeDtypeStruct((depth, rows, n6), F32),
        compiler_params=_cparams("arbitrary", "arbitrary"),
        name="adaln",
    )(vecs, w_mod, b_mod.reshape(depth, 1, n6))


def _rope_tables(L, Lc, dim):
    rows = L // GRID_W
    row = jnp.repeat(jnp.arange(rows, dtype=F32), GRID_W)
    col = jnp.tile(jnp.arange(GRID_W, dtype=F32), rows)
    axis_dim = dim // 2
    n = axis_dim // 2
    inv = jnp.power(ROPE_BASE, -jnp.arange(n, dtype=F32) * 2.0 / axis_dim)
    ar = row[:, None] * inv[None, :]
    ac = col[:, None] * inv[None, :]
    cos = jnp.concatenate([jnp.cos(ar), jnp.cos(ar), jnp.cos(ac), jnp.cos(ac)], axis=-1)
    sin = jnp.concatenate([-jnp.sin(ar), jnp.sin(ar), -jnp.sin(ac), jnp.sin(ac)], axis=-1)
    cos = jnp.concatenate([jnp.ones((Lc, dim), F32), cos], axis=0)
    sin = jnp.concatenate([jnp.zeros((Lc, dim), F32), sin], axis=0)
    rep = LANES // dim
    return jnp.tile(cos, (1, rep)), jnp.tile(sin, (1, rep))


def _rope(x, cos, sin, quarter):
    lane = lax.broadcasted_iota(jnp.int32, x.shape, 1)
    first = (lane % (2 * quarter)) < quarter
    partner = jnp.where(first, pltpu.roll(x, LANES - quarter, 1), pltpu.roll(x, quarter, 1))
    return x * cos + partner * sin


def _inproj_kernel(x_ref, mod_ref, n1g_ref, w_ref, cosa_ref, sina_ref, cosd_ref, sind_ref,
                   gqa_ref, gka_ref, gqd_ref, gkd_ref, lb_ref, g64_ref, g32_ref,
                   qa_ref, ka_ref, va_ref, hqf_ref, hlff_ref, hkf_ref, hqb_ref, hlfb_ref, hkb_ref,
                   hi_ref, hg_ref, qd_ref, kd_ref, vd_ref, *, n_batch, n_ctx_tiles):
    b = pl.program_id(0)
    t = pl.program_id(1)
    mrow = jnp.where(t < n_ctx_tiles, n_batch, b)
    m = mod_ref[pl.ds(mrow, 1), :]
    shift, scale = m[:, 0:D_MODEL], m[:, D_MODEL:2 * D_MODEL]

    x = x_ref[0]
    ms = jnp.mean(x * x, axis=-1, keepdims=True)
    h = (x * lax.rsqrt(ms + EPS) * n1g_ref[...]) * (1.0 + scale) + shift
    h = h.astype(BF16)

    def proj(c0, width):
        return jnp.dot(h, w_ref[:, c0:c0 + width], preferred_element_type=F32)

    g64 = g64_ref[...]
    g32 = g32_ref[...]
    cosa, sina = cosa_ref[...], sina_ref[...]
    cosd, sind = cosd_ref[...], sind_ref[...]

    for c in range(2):
        p = proj(C_AQ + 256 * c, 256)
        qn = p * lax.rsqrt(_group_sumsq(p, g64) * (1.0 / WIN_HEAD_DIM) + EPS)
        qn = qn * (gqa_ref[:, 256 * c:256 * (c + 1)] * (WIN_HEAD_DIM ** -0.5 * LOG2E))
        for j in range(2):
            r = _rope(qn[:, LANES * j:LANES * (j + 1)], cosa, sina, WIN_HEAD_DIM // 4)
            qa_ref[0, :, 256 * c + LANES * j:256 * c + LANES * (j + 1)] = r.astype(BF16)
    p = proj(C_AK, 128)
    kn = p * lax.rsqrt(_group_sumsq(p, g64[:LANES, :LANES]) * (1.0 / WIN_HEAD_DIM) + EPS) * gka_ref[...]
    ka_ref[0] = _rope(kn, cosa, sina, WIN_HEAD_DIM // 4).astype(BF16)
    va_ref[0] = proj(C_AV, 128).astype(BF16)

    for d, (cq, cf, q_ref, lf_ref, k_ref) in enumerate(((C_BQF, C_BFF, hqf_ref, hlff_ref, hkf_ref),
                                                        (C_BQB, C_BFB, hqb_ref, hlfb_ref, hkb_ref))):
        q_ref[0] = proj(cq, 256)
        lb = lb_ref[d:d + 1, :]
        f = lb + (1.0 - lb) * _sigmoid(proj(cf, 256))
        lf_ref[0] = jnp.log(f)
        k_ref[0] = 1.0 - f
    hi_ref[0] = proj(C_BI, 256).astype(BF16)
    g = proj(C_BG, 256)
    hg_ref[0] = g * _sigmoid(g)

    for c0, gain_ref, o_ref, mult in ((C_CQ, gqd_ref, qd_ref, DIFF_QK_DIM ** -0.5 * LOG2E),
                                      (C_CK, gkd_ref, kd_ref, 1.0)):
        p = proj(c0, 256)
        pn = p * lax.rsqrt(_group_sumsq(p, g32) * (1.0 / DIFF_QK_DIM) + EPS) * (gain_ref[...] * mult)
        for j in range(2):
            r = _rope(pn[:, LANES * j:LANES * (j + 1)], cosd, sind, DIFF_QK_DIM // 4)
            o_ref[0, :, LANES * j:LANES * (j + 1)] = r.astype(BF16)
    vd_ref[0] = proj(C_CV, 256).astype(BF16)


def _inproj(xc, mod_l, n1g, w_in, tabs, gains, lb_l, g64, g32, n_batch, Lc):
    B, Lt, D = xc.shape
    nt = Lt // TOK_TILE
    cosa, sina, cosd, sind = tabs
    gqa, gka, gqd, gkd = gains
    full = lambda a: pl.BlockSpec(a.shape, lambda b, t: (0,) * a.ndim)
    tab = pl.BlockSpec((TOK_TILE, LANES), lambda b, t: (t, 0))
    tok = lambda w: pl.BlockSpec((1, TOK_TILE, w), lambda b, t: (b, t, 0))
    widths = (512, 128, 128, 256, 256, 256, 256, 256, 256, 256, 256, 256, 256, 256)
    dtypes = (BF16, BF16, BF16, F32, F32, F32, F32, F32, F32, BF16, F32, BF16, BF16, BF16)
    return pl.pallas_call(
        functools.partial(_inproj_kernel, n_batch=n_batch, n_ctx_tiles=Lc // TOK_TILE),
        grid=(B, nt),
        in_specs=[tok(D), full(mod_l), full(n1g), full(w_in), tab, tab, tab, tab,
                  full(gqa), full(gka), full(gqd), full(gkd), full(lb_l), full(g64), full(g32)],
        out_specs=[tok(w) for w in widths],
        out_shape=[jax.ShapeDtypeStruct((B, Lt, w), dt) for w, dt in zip(widths, dtypes)],
        compiler_params=_cparams("parallel", "arbitrary"),
        name="inproj",
    )(xc, mod_l, n1g, w_in, cosa, sina, cosd, sind, gqa, gka, gqd, gkd, lb_l, g64, g32)


def _gqa_softmax_pv(s_blocks, v_blocks, sinks):
    rows = s_blocks[0].shape[0]
    per_head = rows // len(sinks)
    rid = lax.broadcasted_iota(jnp.int32, (rows, 1), 0)
    sink = jnp.full((rows, 1), sinks[-1], F32)
    for g in range(len(sinks) - 2, -1, -1):
        sink = jnp.where(rid < (g + 1) * per_head, sinks[g], sink)
    s = jnp.concatenate(s_blocks, axis=1)
    m = jnp.maximum(jnp.max(s, axis=-1, keepdims=True), sink)
    e = jnp.exp2(s - m).astype(BF16)
    pv = jnp.dot(e, jnp.concatenate(v_blocks, axis=0), preferred_element_type=F32)
    den = pv[:, WIN_HEAD_DIM:WIN_HEAD_DIM + 1] + jnp.exp2(sink - m)
    return pv[:, :WIN_HEAD_DIM] / den


def _win_attn_kernel(sink_ref, q_ref, kp_ref, kc_ref, kn_ref, kx_ref, vp_ref, vc_ref, vn_ref, vx_ref, o_ref):
    j = pl.program_id(1)
    nb = pl.num_programs(1)
    group = WIN_HEADS // WIN_KV_HEADS
    rows = group * ATT_BLOCK
    t = lax.broadcasted_iota(jnp.int32, (rows, ATT_BLOCK), 0) % ATT_BLOCK
    s = lax.broadcasted_iota(jnp.int32, (rows, ATT_BLOCK), 1)
    mask_prev = (s >= t) & (j > 0)
    mask_next = (s <= t) & (j < nb - 1)
    for hk in range(WIN_KV_HEADS):
        q = q_ref[0, hk * group:(hk + 1) * group].reshape(rows, WIN_HEAD_DIM)
        dot = lambda k_ref: jnp.dot(q, k_ref[0, hk], preferred_element_type=F32)
        blocks = [jnp.where(mask_prev, dot(kp_ref), NEG), dot(kc_ref),
                  jnp.where(mask_next, dot(kn_ref), NEG), dot(kx_ref)]
        o = _gqa_softmax_pv(blocks, [vp_ref[0, hk], vc_ref[0, hk], vn_ref[0, hk], vx_ref[0, hk]],
                            [sink_ref[hk * group + g] for g in range(group)])
        o_ref[0, hk * group:(hk + 1) * group] = o.reshape(group, ATT_BLOCK, WIN_HEAD_DIM).astype(BF16)


def _ctx_attn_kernel(sink_ref, q_ref, kx_ref, vx_ref, o_ref):
    group = WIN_HEADS // WIN_KV_HEADS
    n = q_ref.shape[2]
    for hk in range(WIN_KV_HEADS):
        q = q_ref[0, hk * group:(hk + 1) * group].reshape(group * n, WIN_HEAD_DIM)
        s_ctx = jnp.dot(q, kx_ref[0, hk], preferred_element_type=F32)
        o = _gqa_softmax_pv([s_ctx], [vx_ref[0, hk]], [sink_ref[hk * group + g] for g in range(group)])
        o_ref[0, hk * group:(hk + 1) * group] = o.reshape(group, n, WIN_HEAD_DIM).astype(BF16)


def _win_attention(sink, q_h, kT, v_h, L, Lc, need_ctx):
    B = q_h.shape[0]
    nb = L // ATT_BLOCK
    off = Lc // ATT_BLOCK
    dh = WIN_HEAD_DIM
    smem = pl.BlockSpec(memory_space=pltpu.SMEM)
    kblk = lambda f: pl.BlockSpec((1, WIN_KV_HEADS, dh, ATT_BLOCK), lambda b, j: (b, 0, 0, off + f(j)))
    vblk = lambda f: pl.BlockSpec((1, WIN_KV_HEADS, ATT_BLOCK, LANES), lambda b, j: (b, 0, off + f(j), 0))
    prev = lambda j: jnp.maximum(j - 1, 0)
    cur = lambda j: j
    nxt = lambda j: jnp.minimum(j + 1, nb - 1)
    kctx = pl.BlockSpec((1, WIN_KV_HEADS, dh, Lc), lambda b, j: (b, 0, 0, 0))
    vctx = pl.BlockSpec((1, WIN_KV_HEADS, Lc, LANES), lambda b, j: (b, 0, 0, 0))
    o_lat = pl.pallas_call(
        _win_attn_kernel,
        grid=(B, nb),
        in_specs=[smem, pl.BlockSpec((1, WIN_HEADS, ATT_BLOCK, dh), lambda b, j: (b, 0, off + j, 0)),
                  kblk(prev), kblk(cur), kblk(nxt), kctx, vblk(prev), vblk(cur), vblk(nxt), vctx],
        out_specs=pl.BlockSpec((1, WIN_HEADS, ATT_BLOCK, dh), lambda b, j: (b, 0, j, 0)),
        out_shape=jax.ShapeDtypeStruct((B, WIN_HEADS, L, dh), BF16),
        compiler_params=_cparams("parallel", "arbitrary"),
        name="win_attn",
    )(sink, q_h, kT, kT, kT, kT, v_h, v_h, v_h, v_h)
    if not need_ctx:
        return o_lat, None
    o_ctx = pl.pallas_call(
        _ctx_attn_kernel,
        grid=(B,),
        in_specs=[smem, pl.BlockSpec((1, WIN_HEADS, Lc, dh), lambda b: (b, 0, 0, 0)),
                  pl.BlockSpec((1, WIN_KV_HEADS, dh, Lc), lambda b: (b, 0, 0, 0)),
                  pl.BlockSpec((1, WIN_KV_HEADS, Lc, LANES), lambda b: (b, 0, 0, 0))],
        out_specs=pl.BlockSpec((1, WIN_HEADS, Lc, dh), lambda b: (b, 0, 0, 0)),
        out_shape=jax.ShapeDtypeStruct((B, WIN_HEADS, Lc, dh), BF16),
        compiler_params=_cparams("parallel"),
        name="ctx_attn",
    )(sink, q_h, kT, v_h)
    return o_lat, o_ctx


def _hgrn_kernel(q_ref, lf_ref, k_ref, v_ref, g64_ref, o_ref, st_ref, *, reverse):
    @pl.when(pl.program_id(1) == 0)
    def _():
        st_ref[...] = jnp.zeros_like(st_ref)

    width = HG_HEADS * HG_KEY_DIM
    n_pairs = width // LANES
    n_groups = HG_BLOCK // HG_GROUP
    n_sub = HG_GROUP // HG_SUB
    half = HG_SUB // 2
    ri = lax.broadcasted_iota(jnp.int32, (HG_GROUP, HG_GROUP), 0)
    ci = lax.broadcasted_iota(jnp.int32, (HG_GROUP, HG_GROUP), 1)
    in_sub = (ri // HG_SUB) == (ci // HG_SUB)
    tri = (in_sub & ((ci >= ri) if reverse else (ci <= ri))).astype(BF16)
    rows_full = lax.broadcasted_iota(jnp.int32, (HG_SUB, width), 0)
    rows_half = lax.broadcasted_iota(jnp.int32, (half, width), 0) + (0 if reverse else half)
    same_head = (lax.broadcasted_iota(jnp.int32, (LANES, LANES), 0) // HG_VAL_DIM
                 == lax.broadcasted_iota(jnp.int32, (LANES, LANES), 1) // HG_KEY_DIM)
    g64 = g64_ref[...]
    end_row = 0 if reverse else HG_SUB - 1
    half_lo = 0 if reverse else half

    def is_full(s):
        return (s >= half) if reverse else (s < half)

    def group(gi, carry):
        g = (n_groups - 1 - gi) if reverse else gi
        g0 = pl.multiple_of(g * HG_GROUP, HG_GROUP)
        lf = lf_ref[0, pl.ds(g0, HG_GROUP), :]
        q = q_ref[0, pl.ds(g0, HG_GROUP), :]
        k = k_ref[0, pl.ds(g0, HG_GROUP), :]
        v = v_ref[0, pl.ds(g0, HG_GROUP), :]
        vf = v.astype(F32)

        hi, mid, lo = _split3(lf)
        bcum = (jnp.dot(tri, hi, preferred_element_type=F32) + jnp.dot(tri, mid, preferred_element_type=F32)
                + jnp.dot(tri, lo, preferred_element_type=F32))
        qt = (q * jnp.exp(bcum)).astype(BF16)

        parts = []
        for c in range(n_sub):
            r0 = c * HG_SUB
            bc, qc, kc = bcum[r0:r0 + HG_SUB], q[r0:r0 + HG_SUB], k[r0:r0 + HG_SUB]
            for s in range(HG_SUB):
                if is_full(s):
                    rows, bt, qq = rows_full, bc, qc
                else:
                    rows, bt, qq = rows_half, bc[half_lo:half_lo + half], qc[half_lo:half_lo + half]
                valid = (rows <= s) if reverse else (rows >= s)
                parts.append(jnp.where(valid, qq * jnp.exp(bt - bc[s:s + 1]) * kc[s:s + 1], 0.0))
        a = jnp.dot(jnp.concatenate(parts, axis=0).astype(BF16), g64, preferred_element_type=F32)
        intra = []
        off = 0
        for c in range(n_sub):
            r0 = c * HG_SUB
            o_full = jnp.zeros((HG_SUB, width), F32)
            o_half = jnp.zeros((half, width), F32)
            for s in range(HG_SUB):
                vs = vf[r0 + s:r0 + s + 1]
                if is_full(s):
                    o_full = o_full + a[off:off + HG_SUB] * vs
                    off += HG_SUB
                else:
                    o_half = o_half + a[off:off + half] * vs
                    off += half
            pad = jnp.zeros((half, width), F32)
            intra.append(o_full + jnp.concatenate([o_half, pad] if reverse else [pad, o_half], axis=0))

        sts = [st_ref[p] for p in range(n_pairs)]
        for c in (range(n_sub - 1, -1, -1) if reverse else range(n_sub)):
            r0 = c * HG_SUB
            bc = bcum[r0:r0 + HG_SUB]
            b_end = bc[end_row:end_row + 1]
            khat = (k[r0:r0 + HG_SUB] * jnp.exp(b_end - bc)).astype(BF16)
            dec = jnp.exp(b_end)
            o_parts = []
            for p in range(n_pairs):
                ls = slice(p * LANES, (p + 1) * LANES)
                st_b = jnp.where(same_head, sts[p], 0.0).astype(BF16)
                o_parts.append(lax.dot_general(qt[r0:r0 + HG_SUB, ls], st_b, (((1,), (1,)), ((), ())),
                                               preferred_element_type=F32))
                upd = lax.dot_general(v[r0:r0 + HG_SUB, ls], khat[:, ls], (((0,), (0,)), ((), ())),
                                      preferred_element_type=F32)
                sts[p] = sts[p] * dec[:, ls] + upd
            o_ref[0, pl.ds(g0 + r0, HG_SUB), :] = intra[c] + jnp.concatenate(o_parts, axis=1)
        for p in range(n_pairs):
            st_ref[p] = sts[p]
        return carry

    lax.fori_loop(0, n_groups, group, 0)


def _hgrn_scan(q, lf, k, v, g64, Lc, reverse):
    B, Lt, W = q.shape
    nblk = Lt // HG_BLOCK
    nctx = Lc // HG_BLOCK
    if reverse:
        order = lambda i: jnp.where(i < nctx, nctx - 1 - i, nctx + (nblk - 1 - i))
    else:
        order = lambda i: i
    spec = pl.BlockSpec((1, HG_BLOCK, W), lambda b, i: (b, order(i), 0))
    return pl.pallas_call(
        functools.partial(_hgrn_kernel, reverse=reverse),
        grid=(B, nblk),
        in_specs=[spec, spec, spec, spec, pl.BlockSpec(g64.shape, lambda b, i: (0, 0))],
        out_specs=spec,
        out_shape=jax.ShapeDtypeStruct((B, Lt, W), F32),
        scratch_shapes=[pltpu.VMEM((W // LANES, LANES, LANES), F32)],
        compiler_params=_cparams("parallel", "arbitrary"),
        name="hgrn_bwd" if reverse else "hgrn_fwd",
    )(q, lf, k, v, g64)


def _diff_attn_kernel(lam_ref, q_ref, kT_ref, v_ref, og_ref, o_ref, *, key_block, n_keys, post_scale):
    nkb = n_keys // key_block
    qs = [q_ref[0, hc] for hc in range(2 * DIFF_HEADS)]

    def body(kb, carry):
        ms, accs = carry
        k0 = pl.multiple_of(kb * key_block, key_block)
        new_ms, new_accs = [], []
        for h in range(DIFF_HEADS):
            s = jnp.concatenate(
                [jnp.dot(qs[2 * h + c], kT_ref[0, 2 * h + c, :, pl.ds(k0, key_block)],
                         preferred_element_type=F32) for c in range(2)], axis=0)
            m_new = jnp.maximum(ms[h], jnp.max(s, axis=-1, keepdims=True))
            e = jnp.exp2(s - m_new).astype(BF16)
            pv = jnp.dot(e, v_ref[0, h, pl.ds(k0, key_block), :], preferred_element_type=F32)
            new_accs.append(accs[h] * jnp.exp2(ms[h] - m_new) + pv)
            new_ms.append(m_new)
        return tuple(new_ms), tuple(new_accs)

    m0 = tuple(jnp.full((2 * DIFF_TQ, 1), NEG, F32) for _ in range(DIFF_HEADS))
    acc0 = tuple(jnp.zeros((2 * DIFF_TQ, LANES), F32) for _ in range(DIFF_HEADS))
    _, accs = lax.fori_loop(0, nkb, body, (m0, acc0))
    lam = lam_ref[0]
    for h in range(DIFF_HEADS):
        p = accs[h][:, :DIFF_V_DIM] / accs[h][:, DIFF_V_DIM:DIFF_V_DIM + 1]
        o = p[:DIFF_TQ] - lam * p[DIFF_TQ:]
        ms = jnp.mean(o * o, axis=-1, keepdims=True)
        o_ref[0, h] = (o * lax.rsqrt(ms + EPS) * og_ref[...] * post_scale).astype(BF16)


def _pick_key_block(n):
    best = LANES
    for kb in range(LANES, min(n, 1024) + 1, LANES):
        if n % kb == 0:
            best = kb
    return best


def _diff_attention(lam, q_h, kT, v_aug, og, q_off, n_q, n_keys, post_scale):
    B = q_h.shape[0]
    dq, dv = DIFF_QK_DIM, DIFF_V_DIM
    nq = n_q // DIFF_TQ
    qo = q_off // DIFF_TQ
    return pl.pallas_call(
        functools.partial(_diff_attn_kernel, key_block=_pick_key_block(n_keys), n_keys=n_keys,
                          post_scale=post_scale),
        grid=(B, nq),
        in_specs=[pl.BlockSpec(memory_space=pltpu.SMEM),
                  pl.BlockSpec((1, 2 * DIFF_HEADS, DIFF_TQ, dq), lambda b, i: (b, 0, qo + i, 0)),
                  pl.BlockSpec((1, 2 * DIFF_HEADS, dq, n_keys), lambda b, i: (b, 0, 0, 0)),
                  pl.BlockSpec((1, DIFF_HEADS, n_keys, LANES), lambda b, i: (b, 0, 0, 0)),
                  pl.BlockSpec((1, dv), lambda b, i: (0, 0))],
        out_specs=pl.BlockSpec((1, DIFF_HEADS, DIFF_TQ, dv), lambda b, i: (b, 0, i, 0)),
        out_shape=jax.ShapeDtypeStruct((B, DIFF_HEADS, n_q, dv), BF16),
        compiler_params=_cparams("parallel", "arbitrary"),
        name="diff_attn",
    )(lam, q_h, kT, v_aug, og)


def _outproj_kernel(x_ref, oa_ref, of_ref, ob_ref, gate_ref, od_ref, mod_ref, w_ref, hog_ref, n2g_ref, g64_ref,
                    xm_ref, h2_ref, *, n_batch, n_ctx_tiles, tile_off):
    b = pl.program_id(0)
    t = pl.program_id(1) + tile_off
    mrow = jnp.where(t < n_ctx_tiles, n_batch, b)
    m = mod_ref[pl.ds(mrow, 1), :]
    gate1 = m[:, 2 * D_MODEL:3 * D_MODEL]
    shift2, scale2 = m[:, 3 * D_MODEL:4 * D_MODEL], m[:, 4 * D_MODEL:5 * D_MODEL]

    o = of_ref[0] + ob_ref[0]
    on = o * lax.rsqrt(_group_sumsq(o, g64_ref[...]) * (1.0 / HG_VAL_DIM) + EPS) * hog_ref[...]
    ob = (on * gate_ref[0]).astype(BF16)

    wa, wb = WIN_HEADS * WIN_HEAD_DIM, HG_HEADS * HG_VAL_DIM
    mix = jnp.dot(oa_ref[0], w_ref[0:wa, :], preferred_element_type=F32)
    mix += jnp.dot(ob, w_ref[wa:wa + wb, :], preferred_element_type=F32)
    mix += jnp.dot(od_ref[0], w_ref[wa + wb:, :], preferred_element_type=F32)

    x = x_ref[0] + gate1 * mix
    xm_ref[0] = x
    ms = jnp.mean(x * x, axis=-1, keepdims=True)
    h = (x * lax.rsqrt(ms + EPS) * n2g_ref[...]) * (1.0 + scale2) + shift2
    h2_ref[0] = h.astype(BF16)


def _outproj(xc, oa, of, ob, gate, od, mod_l, w_out, hog, n2g, g64, n_batch, Lc, tile_off, mix_off):
    B, Lt, D = xc.shape
    nt = Lt // TOK_TILE - tile_off
    full = lambda a: pl.BlockSpec(a.shape, lambda b, t: (0,) * a.ndim)
    comb = lambda w: pl.BlockSpec((1, TOK_TILE, w), lambda b, t: (b, t + tile_off, 0))
    part = lambda w: pl.BlockSpec((1, TOK_TILE, w), lambda b, t: (b, t + tile_off - mix_off, 0))
    outs = pl.BlockSpec((1, TOK_TILE, D), lambda b, t: (b, t, 0))
    return pl.pallas_call(
        functools.partial(_outproj_kernel, n_batch=n_batch, n_ctx_tiles=Lc // TOK_TILE, tile_off=tile_off),
        grid=(B, nt),
        in_specs=[comb(D), part(oa.shape[-1]), comb(256), comb(256), comb(256), part(od.shape[-1]),
                  full(mod_l), full(w_out), full(hog), full(n2g), full(g64)],
        out_specs=[outs, outs],
        out_shape=[jax.ShapeDtypeStruct((B, nt * TOK_TILE, D), F32),
                   jax.ShapeDtypeStruct((B, nt * TOK_TILE, D), BF16)],
        compiler_params=_cparams("parallel", "arbitrary"),
        name="outproj",
    )(xc, oa, of, ob, gate, od, mod_l, w_out, hog, n2g, g64)


def _ffn_kernel(x_ref, h_ref, hp_ref, hn_ref, mod_ref, wu_ref, cw_ref, cb_ref, wd_ref, o_ref,
                *, n_batch, n_ctx_tiles, tile_off, n_tiles_total):
    b = pl.program_id(0)
    t = pl.program_id(1) + tile_off
    mrow = jnp.where(t < n_ctx_tiles, n_batch, b)
    gate2 = mod_ref[pl.ds(mrow, 1), :][:, 5 * D_MODEL:6 * D_MODEL]

    prev_ok = jnp.logical_and(t != 0, t != n_ctx_tiles)
    next_ok = jnp.logical_and(t != n_ctx_tiles - 1, t != n_tiles_total - 1)
    hp = jnp.where(prev_ok, hp_ref[0], jnp.zeros_like(hp_ref[0]))
    hn = jnp.where(next_ok, hn_ref[0], jnp.zeros_like(hn_ref[0]))
    hcat = jnp.concatenate([hp, h_ref[0], hn], axis=0)

    lo = BF16_ROWS
    acc = jnp.zeros((TOK_TILE, D_MODEL), F32)
    for c in range(D_FF // FF_CHUNK):
        ys = []
        for base in (0, D_FF):
            c0 = base + c * FF_CHUNK
            u = jnp.dot(hcat, wu_ref[:, c0:c0 + FF_CHUNK], preferred_element_type=F32)
            w = cw_ref[:, c0:c0 + FF_CHUNK]
            y = (cb_ref[:, c0:c0 + FF_CHUNK] + u[lo - 1:lo - 1 + TOK_TILE] * w[0:1]
                 + u[lo:lo + TOK_TILE] * w[1:2] + u[lo + 1:lo + 1 + TOK_TILE] * w[2:3])
            ys.append(y)
        act = (ys[0] * _sigmoid(ys[0]) * ys[1]).astype(BF16)
        acc += jnp.dot(act, wd_ref[c * FF_CHUNK:(c + 1) * FF_CHUNK, :], preferred_element_type=F32)
    o_ref[0] = x_ref[0] + gate2 * acc


def _conv_ffn(xm, h2, mod_l, w_up, conv_w, conv_b, w_down, n_batch, Lc, tile_off, n_tiles_total):
    B, T, D = xm.shape
    nt = T // TOK_TILE
    r = TOK_TILE // BF16_ROWS
    last = T // BF16_ROWS - 1
    full = lambda a: pl.BlockSpec(a.shape, lambda b, t: (0,) * a.ndim)
    tok = pl.BlockSpec((1, TOK_TILE, D), lambda b, t: (b, t, 0))
    halo_prev = pl.BlockSpec((1, BF16_ROWS, D), lambda b, t: (b, jnp.maximum(t * r - 1, 0), 0))
    halo_next = pl.BlockSpec((1, BF16_ROWS, D), lambda b, t: (b, jnp.minimum((t + 1) * r, last), 0))
    return pl.pallas_call(
        functools.partial(_ffn_kernel, n_batch=n_batch, n_ctx_tiles=Lc // TOK_TILE, tile_off=tile_off,
                          n_tiles_total=n_tiles_total),
        grid=(B, nt),
        in_specs=[tok, tok, halo_prev, halo_next, full(mod_l), full(w_up), full(conv_w), full(conv_b),
                  full(w_down)],
        out_specs=tok,
        out_shape=jax.ShapeDtypeStruct((B, T, D), F32),
        compiler_params=_cparams("parallel", "arbitrary"),
        name="conv_ffn",
    )(xm, h2, h2, h2, mod_l, w_up, conv_w, conv_b, w_down)


def _with_ones_column(v):
    pad = jnp.zeros(v.shape[:-1] + (LANES - v.shape[-1],), v.dtype).at[..., 0].set(1.0)
    return jnp.concatenate([v, pad], axis=-1)


def kernel(x, c, ctx, c_ctx, w_mod, b_mod, norm1_g, norm2_g, w_in, win_qnorm_g, win_knorm_g, win_sink,
           hg_lower, hg_onorm_g, diff_qnorm_g, diff_knorm_g, diff_lambda, diff_onorm_g, w_out,
           w_up, conv_w, conv_b, w_down):
    B, L, D = x.shape
    Lc = ctx.shape[1]
    depth = w_mod.shape[0]
    Lt = Lc + L
    assert D == D_MODEL and L % GRID_W == 0
    assert Lc % TOK_TILE == 0 and L % TOK_TILE == 0 and TOK_TILE % HG_BLOCK == 0 and TOK_TILE % ATT_BLOCK == 0
    assert B + 1 <= SUBLANES

    vecs = jnp.zeros((SUBLANES, D), F32).at[:B].set(c.astype(F32)).at[B].set(c_ctx.astype(F32))
    mod = _adaln(vecs, w_mod.astype(F32), b_mod.astype(F32))
    lower, lam_raw = _prep_params(hg_lower, diff_lambda)

    tabs = _rope_tables(L, Lc, WIN_HEAD_DIM) + _rope_tables(L, Lc, DIFF_QK_DIM)
    g64 = _block_diag_ones(256, 64)
    g32 = _block_diag_ones(256, 32)
    row = lambda v, reps: jnp.tile(v.astype(F32), reps).reshape(1, -1)

    xc = jnp.concatenate([ctx.astype(F32), x.astype(F32)], axis=1)
    n_tiles_total = Lt // TOK_TILE
    nct = Lc // TOK_TILE

    for l in range(depth):
        need_ctx = l < depth - 1
        lam_init = 0.8 - 0.6 * math.exp(-0.3 * l)
        gains = (row(win_qnorm_g[l], WIN_HEADS), row(win_knorm_g[l], WIN_KV_HEADS),
                 row(diff_qnorm_g[l], 2 * DIFF_HEADS), row(diff_knorm_g[l], 2 * DIFF_HEADS))
        (qa, ka, va, hqf, hlff, hkf, hqb, hlfb, hkb, hi, hgate, qd, kd, vd) = _inproj(
            xc, mod[l], norm1_g[l].reshape(1, D).astype(F32), w_in[l].astype(BF16), tabs, gains, lower[l],
            g64, g32, B, Lc)

        qa_h = qa.reshape(B, Lt, WIN_HEADS, WIN_HEAD_DIM).transpose(0, 2, 1, 3)
        kaT = ka.reshape(B, Lt, WIN_KV_HEADS, WIN_HEAD_DIM).transpose(0, 2, 3, 1)
        va_h = _with_ones_column(va.reshape(B, Lt, WIN_KV_HEADS, WIN_HEAD_DIM).transpose(0, 2, 1, 3))
        sink = win_sink[l].astype(F32) * LOG2E
        oa_lat, oa_ctx = _win_attention(sink, qa_h, kaT, va_h, L, Lc, need_ctx)

        of = _hgrn_scan(hqf, hlff, hkf, hi, g64, Lc, reverse=False)
        ob = _hgrn_scan(hqb, hlfb, hkb, hi, g64, Lc, reverse=True)

        qd_h = qd.reshape(B, Lt, 2 * DIFF_HEADS, DIFF_QK_DIM).transpose(0, 2, 1, 3)
        kdT = kd.reshape(B, Lt, 2 * DIFF_HEADS, DIFF_QK_DIM).transpose(0, 2, 3, 1)
        vd_aug = _with_ones_column(vd.reshape(B, Lt, DIFF_HEADS, DIFF_V_DIM).transpose(0, 2, 1, 3))
        lam = lam_raw[l, 0, :1] + lam_init
        dog = diff_onorm_g[l].reshape(1, DIFF_V_DIM).astype(F32)
        od_lat = _diff_attention(lam, qd_h, kdT, vd_aug, dog, Lc, L, Lt, 1.0 - lam_init)

        def token_major(o):
            return o.transpose(0, 2, 1, 3).reshape(B, o.shape[2], -1)

        if need_ctx:
            od_ctx = _diff_attention(lam, qd_h, kdT, vd_aug, dog, 0, Lc, Lc, 1.0 - lam_init)
            oa = jnp.concatenate([token_major(oa_ctx), token_major(oa_lat)], axis=1)
            od = jnp.concatenate([token_major(od_ctx), token_major(od_lat)], axis=1)
            tile_off, mix_off = 0, 0
        else:
            oa, od = token_major(oa_lat), token_major(od_lat)
            tile_off, mix_off = nct, nct

        xm, h2 = _outproj(xc, oa, of, ob, hgate, od, mod[l], w_out[l].astype(BF16),
                          row(hg_onorm_g[l], HG_HEADS), norm2_g[l].reshape(1, D).astype(F32), g64,
                          B, Lc, tile_off, mix_off)
        xc = _conv_ffn(xm, h2, mod[l], w_up[l].astype(BF16), conv_w[l].astype(F32),
                       conv_b[l].reshape(1, -1).astype(F32), w_down[l].astype(BF16),
                       B, Lc, tile_off, n_tiles_total)

    return xc
```

```python
import functools
import math

import numpy as np
import jax
import jax.numpy as jnp
from jax import lax
from jax.experimental import pallas as pl
from jax.experimental.pallas import tpu as pltpu

F32 = jnp.float32
BF16 = jnp.bfloat16

D_MODEL = 1024
GRID_W = 64
WIN_HEADS, WIN_KV_HEADS, WIN_HEAD_DIM = 8, 2, 64
WINDOW = 128
HG_HEADS, HG_KEY_DIM, HG_VAL_DIM = 4, 64, 64
DIFF_HEADS, DIFF_QK_DIM, DIFF_V_DIM = 4, 32, 64
D_FF = 11 * D_MODEL // 4
ROPE_BASE = 10000.0
EPS = 1e-6
LOG2E = math.log2(math.e)

_SPLITS = (512, 128, 128, 256, 256, 256, 256, 256, 256, 256, 256, 256)
_OFF = np.concatenate([[0], np.cumsum(_SPLITS)]).tolist()
IN_WIDTH = _OFF[-1]
(C_AQ, C_AK, C_AV, C_BQF, C_BFF, C_BQB, C_BFB, C_BI, C_BG, C_CQ, C_CK, C_CV) = _OFF[:-1]

LANES = 128
SUBLANES = 8
BF16_ROWS = 16
VMEM_LIMIT = 48 * 1024 * 1024

TOK_TILE = 256
ATT_BLOCK = 128
DIFF_TQ = 256
DIFF_VT_ROWS = 80
HG_BLOCK = 256
HG_GROUP = 128
HG_SUB = 16
FF_CHUNK = 256
NEG = -1e30


def _cparams(*sem):
    return pltpu.CompilerParams(dimension_semantics=sem, vmem_limit_bytes=VMEM_LIMIT)


def _sigmoid(x):
    return 1.0 / (1.0 + jnp.exp(-x))


def _split3(x):
    hi = x.astype(BF16)
    r = x - hi.astype(F32)
    mid = r.astype(BF16)
    lo = (r - mid.astype(F32)).astype(BF16)
    return hi, mid, lo


def _group_sumsq(x, g):
    sq = x * x
    hi = sq.astype(BF16)
    lo = (sq - hi.astype(F32)).astype(BF16)
    return jnp.dot(hi, g, preferred_element_type=F32) + jnp.dot(lo, g, preferred_element_type=F32)


def _block_diag_ones(width, group):
    i = np.arange(width)
    return jnp.asarray((i[:, None] // group) == (i[None, :] // group), dtype=BF16)


def _prep_kernel(hl_ref, dl_ref, lb_ref, lam_ref, *, depth):
    raw = [hl_ref[l] for l in range(depth)]
    mx = raw[0]
    for l in range(1, depth):
        mx = jnp.maximum(mx, raw[l])
    ex = [jnp.exp(r - mx) for r in raw]
    tot = ex[0]
    for l in range(1, depth):
        tot = tot + ex[l]
    p = [e / tot for e in ex]
    run = p[0]
    lb_ref[0] = run - p[0]
    for l in range(1, depth):
        run = run + p[l]
        lb_ref[l] = run - p[0]
    for l in range(depth):
        d = dl_ref[l]
        a = jnp.sum(d[0:1] * d[1:2], axis=-1, keepdims=True)
        c = jnp.sum(d[2:3] * d[3:4], axis=-1, keepdims=True)
        lam_ref[l] = jnp.broadcast_to(jnp.exp(a) - jnp.exp(c), (1, LANES))


def _prep_params(hg_lower, diff_lambda):
    depth = hg_lower.shape[0]
    return pl.pallas_call(
        functools.partial(_prep_kernel, depth=depth),
        out_shape=(jax.ShapeDtypeStruct(hg_lower.shape, F32),
                   jax.ShapeDtypeStruct((depth, 1, LANES), F32)),
        name="prep_params",
    )(hg_lower.astype(F32), diff_lambda.astype(F32))


def _adaln_kernel(v_ref, w_ref, b_ref, o_ref):
    v = v_ref[...]
    a = v * _sigmoid(v)
    a_hi, a_mid, _ = _split3(a)
    w = w_ref[0]
    w_hi, w_mid, _ = _split3(w)
    acc = jnp.dot(a_hi, w_hi, preferred_element_type=F32)
    acc += jnp.dot(a_mid, w_hi, preferred_element_type=F32)
    acc += jnp.dot(a_hi, w_mid, preferred_element_type=F32)
    o_ref[0] = acc + b_ref[0]


def _adaln(vecs, w_mod, b_mod):
    depth, d, n6 = w_mod.shape
    rows = vecs.shape[0]
    cb = 768
    assert n6 % cb == 0
    return pl.pallas_call(
        _adaln_kernel,
        grid=(depth, n6 // cb),
        in_specs=[pl.BlockSpec((rows, d), lambda l, j: (0, 0)),
                  pl.BlockSpec((1, d, cb), lambda l, j: (l, 0, j)),
                  pl.BlockSpec((1, 1, cb), lambda l, j: (l, 0, j))],
        out_specs=pl.BlockSpec((1, rows, cb), lambda l, j: (l, 0, j)),
        out_shape=jax.ShapeDtypeStruct((depth, rows, n6), F32),
        compiler_params=_cparams("arbitrary", "arbitrary"),
        name="adaln",
    )(vecs, w_mod, b_mod.reshape(depth, 1, n6))


def _rope_tables(L, Lc, dim):
    rows = L // GRID_W
    row = jnp.repeat(jnp.arange(rows, dtype=F32), GRID_W)
    col = jnp.tile(jnp.arange(GRID_W, dtype=F32), rows)
    axis_dim = dim // 2
    n = axis_dim // 2
    inv = jnp.power(ROPE_BASE, -jnp.arange(n, dtype=F32) * 2.0 / axis_dim)
    ar = row[:, None] * inv[None, :]
    ac = col[:, None] * inv[None, :]
    cos = jnp.concatenate([jnp.cos(ar), jnp.cos(ar), jnp.cos(ac), jnp.cos(ac)], axis=-1)
    sin = jnp.concatenate([-jnp.sin(ar), jnp.sin(ar), -jnp.sin(ac), jnp.sin(ac)], axis=-1)
    cos = jnp.concatenate([jnp.ones((Lc, dim), F32), cos], axis=0)
    sin = jnp.concatenate([jnp.zeros((Lc, dim), F32), sin], axis=0)
    rep = LANES // dim
    return jnp.tile(cos, (1, rep)), jnp.tile(sin, (1, rep))


def _rope(x, cos, sin, quarter):
    lane = lax.broadcasted_iota(jnp.int32, x.shape, 1)
    first = (lane % (2 * quarter)) < quarter
    partner = jnp.where(first, pltpu.roll(x, LANES - quarter, 1), pltpu.roll(x, quarter, 1))
    return x * cos + partner * sin


def _inproj_kernel(x_ref, mod_ref, n1g_ref, w_ref, cosa_ref, sina_ref, cosd_ref, sind_ref,
                   gqa_ref, gka_ref, gqd_ref, gkd_ref, lb_ref, g64_ref, g32_ref,
                   qa_ref, ka_ref, va_ref, hqf_ref, hlff_ref, hkf_ref, hqb_ref, hlfb_ref, hkb_ref,
                   hi_ref, hg_ref, qd_ref, kd_ref, vd_ref, *, n_batch, n_ctx_tiles):
    b = pl.program_id(0)
    t = pl.program_id(1)
    mrow = jnp.where(t < n_ctx_tiles, n_batch, b)
    m = mod_ref[pl.ds(mrow, 1), :]
    shift, scale = m[:, 0:D_MODEL], m[:, D_MODEL:2 * D_MODEL]

    x = x_ref[0]
    ms = jnp.mean(x * x, axis=-1, keepdims=True)
    h = (x * lax.rsqrt(ms + EPS) * n1g_ref[...]) * (1.0 + scale) + shift
    h = h.astype(BF16)

    def proj(c0, width):
        return jnp.dot(h, w_ref[:, c0:c0 + width], preferred_element_type=F32)

    g64 = g64_ref[...]
    g32 = g32_ref[...]
    cosa, sina = cosa_ref[...], sina_ref[...]
    cosd, sind = cosd_ref[...], sind_ref[...]

    for c in range(2):
        p = proj(C_AQ + 256 * c, 256)
        qn = p * lax.rsqrt(_group_sumsq(p, g64) * (1.0 / WIN_HEAD_DIM) + EPS)
        qn = qn * (gqa_ref[:, 256 * c:256 * (c + 1)] * (WIN_HEAD_DIM ** -0.5 * LOG2E))
        for j in range(2):
            r = _rope(qn[:, LANES * j:LANES * (j + 1)], cosa, sina, WIN_HEAD_DIM // 4)
            qa_ref[0, :, 256 * c + LANES * j:256 * c + LANES * (j + 1)] = r.astype(BF16)
    p = proj(C_AK, 128)
    kn = p * lax.rsqrt(_group_sumsq(p, g64[:LANES, :LANES]) * (1.0 / WIN_HEAD_DIM) + EPS) * gka_ref[...]
    ka_ref[0] = _rope(kn, cosa, sina, WIN_HEAD_DIM // 4).astype(BF16)
    va_ref[0] = proj(C_AV, 128).astype(BF16)

    for d, (cq, cf, q_ref, lf_ref, k_ref) in enumerate(((C_BQF, C_BFF, hqf_ref, hlff_ref, hkf_ref),
                                                        (C_BQB, C_BFB, hqb_ref, hlfb_ref, hkb_ref))):
        q_ref[0] = proj(cq, 256)
        lb = lb_ref[d:d + 1, :]
        f = lb + (1.0 - lb) * _sigmoid(proj(cf, 256))
        lf_ref[0] = jnp.log(f)
        k_ref[0] = 1.0 - f
    hi_ref[0] = proj(C_BI, 256).astype(BF16)
    g = proj(C_BG, 256)
    hg_ref[0] = g * _sigmoid(g)

    for c0, gain_ref, o_ref, mult in ((C_CQ, gqd_ref, qd_ref, DIFF_QK_DIM ** -0.5 * LOG2E),
                                      (C_CK, gkd_ref, kd_ref, 1.0)):
        p = proj(c0, 256)
        pn = p * lax.rsqrt(_group_sumsq(p, g32) * (1.0 / DIFF_QK_DIM) + EPS) * (gain_ref[...] * mult)
        for j in range(2):
            r = _rope(pn[:, LANES * j:LANES * (j + 1)], cosd, sind, DIFF_QK_DIM // 4)
            o_ref[0, :, LANES * j:LANES * (j + 1)] = r.astype(BF16)
    vd_ref[0] = proj(C_CV, 256).astype(BF16)


def _inproj(xc, mod_l, n1g, w_in, tabs, gains, lb_l, g64, g32, n_batch, Lc):
    B, Lt, D = xc.shape
    nt = Lt // TOK_TILE
    cosa, sina, cosd, sind = tabs
    gqa, gka, gqd, gkd = gains
    full = lambda a: pl.BlockSpec(a.shape, lambda b, t: (0,) * a.ndim)
    tab = pl.BlockSpec((TOK_TILE, LANES), lambda b, t: (t, 0))
    tok = lambda w: pl.BlockSpec((1, TOK_TILE, w), lambda b, t: (b, t, 0))
    widths = (512, 128, 128, 256, 256, 256, 256, 256, 256, 256, 256, 256, 256, 256)
    dtypes = (BF16, BF16, BF16, F32, F32, F32, F32, F32, F32, BF16, F32, BF16, BF16, BF16)
    return pl.pallas_call(
        functools.partial(_inproj_kernel, n_batch=n_batch, n_ctx_tiles=Lc // TOK_TILE),
        grid=(B, nt),
        in_specs=[tok(D), full(mod_l), full(n1g), full(w_in), tab, tab, tab, tab,
                  full(gqa), full(gka), full(gqd), full(gkd), full(lb_l), full(g64), full(g32)],
        out_specs=[tok(w) for w in widths],
        out_shape=[jax.ShapeDtypeStruct((B, Lt, w), dt) for w, dt in zip(widths, dtypes)],
        compiler_params=_cparams("parallel", "arbitrary"),
        name="inproj",
    )(xc, mod_l, n1g, w_in, cosa, sina, cosd, sind, gqa, gka, gqd, gkd, lb_l, g64, g32)


def _gqa_softmax_pv(s_blocks, v_blocks, sinks):
    rows = s_blocks[0].shape[0]
    per_head = rows // len(sinks)
    rid = lax.broadcasted_iota(jnp.int32, (rows, 1), 0)
    sink = jnp.full((rows, 1), sinks[-1], F32)
    for g in range(len(sinks) - 2, -1, -1):
        sink = jnp.where(rid < (g + 1) * per_head, sinks[g], sink)
    s = jnp.concatenate(s_blocks, axis=1)
    m = jnp.maximum(jnp.max(s, axis=-1, keepdims=True), sink)
    e = jnp.exp2(s - m).astype(BF16)
    pv = jnp.dot(e, jnp.concatenate(v_blocks, axis=0), preferred_element_type=F32)
    den = pv[:, WIN_HEAD_DIM:WIN_HEAD_DIM + 1] + jnp.exp2(sink - m)
    return pv[:, :WIN_HEAD_DIM] / den


def _win_attn_kernel(sink_ref, q_ref, kp_ref, kc_ref, kn_ref, kx_ref, vp_ref, vc_ref, vn_ref, vx_ref, o_ref):
    j = pl.program_id(1)
    nb = pl.num_programs(1)
    group = WIN_HEADS // WIN_KV_HEADS
    rows = group * ATT_BLOCK
    t = lax.broadcasted_iota(jnp.int32, (rows, ATT_BLOCK), 0) % ATT_BLOCK
    s = lax.broadcasted_iota(jnp.int32, (rows, ATT_BLOCK), 1)
    mask_prev = (s >= t) & (j > 0)
    mask_next = (s <= t) & (j < nb - 1)
    for hk in range(WIN_KV_HEADS):
        q = q_ref[0, hk * group:(hk + 1) * group].reshape(rows, WIN_HEAD_DIM)
        dot = lambda k_ref: jnp.dot(q, k_ref[0, hk], preferred_element_type=F32)
        blocks = [jnp.where(mask_prev, dot(kp_ref), NEG), dot(kc_ref),
                  jnp.where(mask_next, dot(kn_ref), NEG), dot(kx_ref)]
        o = _gqa_softmax_pv(blocks, [vp_ref[0, hk], vc_ref[0, hk], vn_ref[0, hk], vx_ref[0, hk]],
                            [sink_ref[hk * group + g] for g in range(group)])
        o_ref[0, hk * group:(hk + 1) * group] = o.reshape(group, ATT_BLOCK, WIN_HEAD_DIM).astype(BF16)


def _ctx_attn_kernel(sink_ref, q_ref, kx_ref, vx_ref, o_ref):
    group = WIN_HEADS // WIN_KV_HEADS
    n = q_ref.shape[2]
    for hk in range(WIN_KV_HEADS):
        q = q_ref[0, hk * group:(hk + 1) * group].reshape(group * n, WIN_HEAD_DIM)
        s_ctx = jnp.dot(q, kx_ref[0, hk], preferred_element_type=F32)
        o = _gqa_softmax_pv([s_ctx], [vx_ref[0, hk]], [sink_ref[hk * group + g] for g in range(group)])
        o_ref[0, hk * group:(hk + 1) * group] = o.reshape(group, n, WIN_HEAD_DIM).astype(BF16)


def _win_attention(sink, q_h, kT, v_h, L, Lc, need_ctx):
    B = q_h.shape[0]
    nb = L // ATT_BLOCK
    off = Lc // ATT_BLOCK
    dh = WIN_HEAD_DIM
    smem = pl.BlockSpec(memory_space=pltpu.SMEM)
    kblk = lambda f: pl.BlockSpec((1, WIN_KV_HEADS, dh, ATT_BLOCK), lambda b, j: (b, 0, 0, off + f(j)))
    vblk = lambda f: pl.BlockSpec((1, WIN_KV_HEADS, ATT_BLOCK, LANES), lambda b, j: (b, 0, off + f(j), 0))
    prev = lambda j: jnp.maximum(j - 1, 0)
    cur = lambda j: j
    nxt = lambda j: jnp.minimum(j + 1, nb - 1)
    kctx = pl.BlockSpec((1, WIN_KV_HEADS, dh, Lc), lambda b, j: (b, 0, 0, 0))
    vctx = pl.BlockSpec((1, WIN_KV_HEADS, Lc, LANES), lambda b, j: (b, 0, 0, 0))
    o_lat = pl.pallas_call(
        _win_attn_kernel,
        grid=(B, nb),
        in_specs=[smem, pl.BlockSpec((1, WIN_HEADS, ATT_BLOCK, dh), lambda b, j: (b, 0, off + j, 0)),
                  kblk(prev), kblk(cur), kblk(nxt), kctx, vblk(prev), vblk(cur), vblk(nxt), vctx],
        out_specs=pl.BlockSpec((1, WIN_HEADS, ATT_BLOCK, dh), lambda b, j: (b, 0, j, 0)),
        out_shape=jax.ShapeDtypeStruct((B, WIN_HEADS, L, dh), BF16),
        compiler_params=_cparams("parallel", "arbitrary"),
        name="win_attn",
    )(sink, q_h, kT, kT, kT, kT, v_h, v_h, v_h, v_h)
    if not need_ctx:
        return o_lat, None
    o_ctx = pl.pallas_call(
        _ctx_attn_kernel,
        grid=(B,),
        in_specs=[smem, pl.BlockSpec((1, WIN_HEADS, Lc, dh), lambda b: (b, 0, 0, 0)),
                  pl.BlockSpec((1, WIN_KV_HEADS, dh, Lc), lambda b: (b, 0, 0, 0)),
                  pl.BlockSpec((1, WIN_KV_HEADS, Lc, LANES), lambda b: (b, 0, 0, 0))],
        out_specs=pl.BlockSpec((1, WIN_HEADS, Lc, dh), lambda b: (b, 0, 0, 0)),
        out_shape=jax.ShapeDtypeStruct((B, WIN_HEADS, Lc, dh), BF16),
        compiler_params=_cparams("parallel"),
        name="ctx_attn",
    )(sink, q_h, kT, v_h)
    return o_lat, o_ctx


def _hgrn_kernel(q_ref, lf_ref, k_ref, v_ref, g64_ref, o_ref, st_ref, *, reverse):
    @pl.when(pl.program_id(1) == 0)
    def _():
        st_ref[...] = jnp.zeros_like(st_ref)

    width = HG_HEADS * HG_KEY_DIM
    n_pairs = width // LANES
    n_groups = HG_BLOCK // HG_GROUP
    n_sub = HG_GROUP // HG_SUB
    half = HG_SUB // 2
    ri = lax.broadcasted_iota(jnp.int32, (HG_GROUP, HG_GROUP), 0)
    ci = lax.broadcasted_iota(jnp.int32, (HG_GROUP, HG_GROUP), 1)
    in_sub = (ri // HG_SUB) == (ci // HG_SUB)
    tri = (in_sub & ((ci >= ri) if reverse else (ci <= ri))).astype(BF16)
    rows_full = lax.broadcasted_iota(jnp.int32, (HG_SUB, width), 0)
    rows_half = lax.broadcasted_iota(jnp.int32, (half, width), 0) + (0 if reverse else half)
    same_head = (lax.broadcasted_iota(jnp.int32, (LANES, LANES), 0) // HG_VAL_DIM
                 == lax.broadcasted_iota(jnp.int32, (LANES, LANES), 1) // HG_KEY_DIM)
    g64 = g64_ref[...]
    end_row = 0 if reverse else HG_SUB - 1
    half_lo = 0 if reverse else half

    def is_full(s):
        return (s >= half) if reverse else (s < half)

    def group(gi, carry):
        g = (n_groups - 1 - gi) if reverse else gi
        g0 = pl.multiple_of(g * HG_GROUP, HG_GROUP)
        lf = lf_ref[0, pl.ds(g0, HG_GROUP), :]
        q = q_ref[0, pl.ds(g0, HG_GROUP), :]
        k = k_ref[0, pl.ds(g0, HG_GROUP), :]
        v = v_ref[0, pl.ds(g0, HG_GROUP), :]
        vf = v.astype(F32)

        hi, mid, lo = _split3(lf)
        bcum = (jnp.dot(tri, hi, preferred_element_type=F32) + jnp.dot(tri, mid, preferred_element_type=F32)
                + jnp.dot(tri, lo, preferred_element_type=F32))
        qt = (q * jnp.exp(bcum)).astype(BF16)

        parts = []
        for c in range(n_sub):
            r0 = c * HG_SUB
            bc, qc, kc = bcum[r0:r0 + HG_SUB], q[r0:r0 + HG_SUB], k[r0:r0 + HG_SUB]
            for s in range(HG_SUB):
                if is_full(s):
                    rows, bt, qq = rows_full, bc, qc
                else:
                    rows, bt, qq = rows_half, bc[half_lo:half_lo + half], qc[half_lo:half_lo + half]
                valid = (rows <= s) if reverse else (rows >= s)
                parts.append(jnp.where(valid, qq * jnp.exp(bt - bc[s:s + 1]) * kc[s:s + 1], 0.0))
        a = jnp.dot(jnp.concatenate(parts, axis=0).astype(BF16), g64, preferred_element_type=F32)
        intra = []
        off = 0
        for c in range(n_sub):
            r0 = c * HG_SUB
            o_full = jnp.zeros((HG_SUB, width), F32)
            o_half = jnp.zeros((half, width), F32)
            for s in range(HG_SUB):
                vs = vf[r0 + s:r0 + s + 1]
                if is_full(s):
                    o_full = o_full + a[off:off + HG_SUB] * vs
                    off += HG_SUB
                else:
                    o_half = o_half + a[off:off + half] * vs
                    off += half
            pad = jnp.zeros((half, width), F32)
            intra.append(o_full + jnp.concatenate([o_half, pad] if reverse else [pad, o_half], axis=0))

        sts = [st_ref[p] for p in range(n_pairs)]
        for c in (range(n_sub - 1, -1, -1) if reverse else range(n_sub)):
            r0 = c * HG_SUB
            bc = bcum[r0:r0 + HG_SUB]
            b_end = bc[end_row:end_row + 1]
            khat = (k[r0:r0 + HG_SUB] * jnp.exp(b_end - bc)).astype(BF16)
            dec = jnp.exp(b_end)
            o_parts = []
            for p in range(n_pairs):
                ls = slice(p * LANES, (p + 1) * LANES)
                st_b = jnp.where(same_head, sts[p], 0.0).astype(BF16)
                o_parts.append(lax.dot_general(qt[r0:r0 + HG_SUB, ls], st_b, (((1,), (1,)), ((), ())),
                                               preferred_element_type=F32))
                upd = lax.dot_general(v[r0:r0 + HG_SUB, ls], khat[:, ls], (((0,), (0,)), ((), ())),
                                      preferred_element_type=F32)
                sts[p] = sts[p] * dec[:, ls] + upd
            o_ref[0, pl.ds(g0 + r0, HG_SUB), :] = intra[c] + jnp.concatenate(o_parts, axis=1)
        for p in range(n_pairs):
            st_ref[p] = sts[p]
        return carry

    lax.fori_loop(0, n_groups, group, 0)


def _hgrn_scan(q, lf, k, v, g64, Lc, reverse):
    B, Lt, W = q.shape
    nblk = Lt // HG_BLOCK
    nctx = Lc // HG_BLOCK
    if reverse:
        order = lambda i: jnp.where(i < nctx, nctx - 1 - i, nctx + (nblk - 1 - i))
    else:
        order = lambda i: i
    spec = pl.BlockSpec((1, HG_BLOCK, W), lambda b, i: (b, order(i), 0))
    return pl.pallas_call(
        functools.partial(_hgrn_kernel, reverse=reverse),
        grid=(B, nblk),
        in_specs=[spec, spec, spec, spec, pl.BlockSpec(g64.shape, lambda b, i: (0, 0))],
        out_specs=spec,
        out_shape=jax.ShapeDtypeStruct((B, Lt, W), F32),
        scratch_shapes=[pltpu.VMEM((W // LANES, LANES, LANES), F32)],
        compiler_params=_cparams("parallel", "arbitrary"),
        name="hgrn_bwd" if reverse else "hgrn_fwd",
    )(q, lf, k, v, g64)


def _diff_attn_kernel(lam_ref, qT_ref, k_ref, vT_ref, og_ref, o_ref, *, key_block, n_keys, post_scale):
    nkb = n_keys // key_block
    n_chain = 2 * DIFF_HEADS
    per_half = LANES // DIFF_QK_DIM
    band = lax.broadcasted_iota(jnp.int32, (LANES, DIFF_TQ), 0) // DIFF_QK_DIM
    ws = []
    for j in range(n_chain):
        half = qT_ref[0, (j // per_half) * LANES:(j // per_half + 1) * LANES, :]
        ws.append(jnp.where(band == j % per_half, half, jnp.zeros_like(half)))

    def body(kb, carry):
        ms, accs = carry
        k0 = pl.multiple_of(kb * key_block, key_block)
        kblks = [k_ref[0, pl.ds(k0, key_block), p * LANES:(p + 1) * LANES] for p in range(n_chain // per_half)]
        sTs = [jnp.dot(kblks[j // per_half], ws[j], preferred_element_type=F32) for j in range(n_chain)]
        m_new = [jnp.maximum(ms[j], jnp.max(sTs[j], axis=0, keepdims=True)) for j in range(n_chain)]
        eTs = [jnp.exp2(sTs[j] - m_new[j]).astype(BF16) for j in range(n_chain)]
        vTs = [vT_ref[0, h, :, pl.ds(k0, key_block)] for h in range(DIFF_HEADS)]
        pvs = [jnp.dot(vTs[j // 2], eTs[j], preferred_element_type=F32) for j in range(n_chain)]
        accs = tuple(accs[j] * jnp.exp2(ms[j] - m_new[j]) + pvs[j] for j in range(n_chain))
        return tuple(m_new), accs

    m0 = tuple(jnp.full((1, DIFF_TQ), NEG, F32) for _ in range(n_chain))
    acc0 = tuple(jnp.zeros((DIFF_VT_ROWS, DIFF_TQ), F32) for _ in range(n_chain))
    _, accs = lax.fori_loop(0, nkb, body, (m0, acc0))
    lam = lam_ref[0]
    for h in range(DIFF_HEADS):
        a0, a1 = accs[2 * h], accs[2 * h + 1]
        o = (a0[:DIFF_V_DIM] / a0[DIFF_V_DIM:DIFF_V_DIM + 1]
             - lam * (a1[:DIFF_V_DIM] / a1[DIFF_V_DIM:DIFF_V_DIM + 1]))
        ms = jnp.mean(o * o, axis=0, keepdims=True)
        o_ref[0, h] = (o * lax.rsqrt(ms + EPS) * (og_ref[...] * post_scale)).astype(BF16)


def _pick_key_block(n):
    best = LANES
    for kb in range(LANES, min(n, 1024) + 1, LANES):
        if n % kb == 0:
            best = kb
    return best


def _diff_attention(lam, qT, k, vT_aug, og_col, q_off, n_q, n_keys, post_scale):
    B = qT.shape[0]
    width = 2 * DIFF_HEADS * DIFF_QK_DIM
    nq = n_q // DIFF_TQ
    qo = q_off // DIFF_TQ
    n_chain = 2 * DIFF_HEADS
    return pl.pallas_call(
        functools.partial(_diff_attn_kernel, key_block=_pick_key_block(n_keys), n_keys=n_keys,
                          post_scale=post_scale),
        grid=(B, nq),
        in_specs=[pl.BlockSpec(memory_space=pltpu.SMEM),
                  pl.BlockSpec((1, width, DIFF_TQ), lambda b, i: (b, 0, qo + i)),
                  pl.BlockSpec((1, n_keys, width), lambda b, i: (b, 0, 0)),
                  pl.BlockSpec((1, DIFF_HEADS, DIFF_VT_ROWS, n_keys), lambda b, i: (b, 0, 0, 0)),
                  pl.BlockSpec((DIFF_V_DIM, 1), lambda b, i: (0, 0))],
        out_specs=pl.BlockSpec((1, DIFF_HEADS, DIFF_V_DIM, DIFF_TQ), lambda b, i: (b, 0, 0, i)),
        out_shape=jax.ShapeDtypeStruct((B, DIFF_HEADS, DIFF_V_DIM, n_q), BF16),
        compiler_params=_cparams("parallel", "arbitrary"),
        name="diff_attn",
    )(lam, qT, k, vT_aug, og_col)


def _outproj_kernel(x_ref, oa_ref, of_ref, ob_ref, gate_ref, od_ref, mod_ref, w_ref, hog_ref, n2g_ref, g64_ref,
                    xm_ref, h2_ref, *, n_batch, n_ctx_tiles, tile_off):
    b = pl.program_id(0)
    t = pl.program_id(1) + tile_off
    mrow = jnp.where(t < n_ctx_tiles, n_batch, b)
    m = mod_ref[pl.ds(mrow, 1), :]
    gate1 = m[:, 2 * D_MODEL:3 * D_MODEL]
    shift2, scale2 = m[:, 3 * D_MODEL:4 * D_MODEL], m[:, 4 * D_MODEL:5 * D_MODEL]

    o = of_ref[0] + ob_ref[0]
    on = o * lax.rsqrt(_group_sumsq(o, g64_ref[...]) * (1.0 / HG_VAL_DIM) + EPS) * hog_ref[...]
    ob = (on * gate_ref[0]).astype(BF16)

    wa, wb = WIN_HEADS * WIN_HEAD_DIM, HG_HEADS * HG_VAL_DIM
    mix = jnp.dot(oa_ref[0], w_ref[0:wa, :], preferred_element_type=F32)
    mix += jnp.dot(ob, w_ref[wa:wa + wb, :], preferred_element_type=F32)
    mix += jnp.dot(od_ref[0], w_ref[wa + wb:, :], preferred_element_type=F32)

    x = x_ref[0] + gate1 * mix
    xm_ref[0] = x
    ms = jnp.mean(x * x, axis=-1, keepdims=True)
    h = (x * lax.rsqrt(ms + EPS) * n2g_ref[...]) * (1.0 + scale2) + shift2
    h2_ref[0] = h.astype(BF16)


def _outproj(xc, oa, of, ob, gate, od, mod_l, w_out, hog, n2g, g64, n_batch, Lc, tile_off, mix_off):
    B, Lt, D = xc.shape
    nt = Lt // TOK_TILE - tile_off
    full = lambda a: pl.BlockSpec(a.shape, lambda b, t: (0,) * a.ndim)
    comb = lambda w: pl.BlockSpec((1, TOK_TILE, w), lambda b, t: (b, t + tile_off, 0))
    part = lambda w: pl.BlockSpec((1, TOK_TILE, w), lambda b, t: (b, t + tile_off - mix_off, 0))
    outs = pl.BlockSpec((1, TOK_TILE, D), lambda b, t: (b, t, 0))
    return pl.pallas_call(
        functools.partial(_outproj_kernel, n_batch=n_batch, n_ctx_tiles=Lc // TOK_TILE, tile_off=tile_off),
        grid=(B, nt),
        in_specs=[comb(D), part(oa.shape[-1]), comb(256), comb(256), comb(256), part(od.shape[-1]),
                  full(mod_l), full(w_out), full(hog), full(n2g), full(g64)],
        out_specs=[outs, outs],
        out_shape=[jax.ShapeDtypeStruct((B, nt * TOK_TILE, D), F32),
                   jax.ShapeDtypeStruct((B, nt * TOK_TILE, D), BF16)],
        compiler_params=_cparams("parallel", "arbitrary"),
        name="outproj",
    )(xc, oa, of, ob, gate, od, mod_l, w_out, hog, n2g, g64)


def _ffn_kernel(x_ref, h_ref, hp_ref, hn_ref, mod_ref, wu_ref, cw_ref, cb_ref, wd_ref, o_ref,
                *, n_batch, n_ctx_tiles, tile_off, n_tiles_total):
    b = pl.program_id(0)
    t = pl.program_id(1) + tile_off
    mrow = jnp.where(t < n_ctx_tiles, n_batch, b)
    gate2 = mod_ref[pl.ds(mrow, 1), :][:, 5 * D_MODEL:6 * D_MODEL]

    prev_ok = jnp.logical_and(t != 0, t != n_ctx_tiles)
    next_ok = jnp.logical_and(t != n_ctx_tiles - 1, t != n_tiles_total - 1)
    hp = jnp.where(prev_ok, hp_ref[0], jnp.zeros_like(hp_ref[0]))
    hn = jnp.where(next_ok, hn_ref[0], jnp.zeros_like(hn_ref[0]))
    hcat = jnp.concatenate([hp, h_ref[0], hn], axis=0)

    lo = BF16_ROWS
    acc = jnp.zeros((TOK_TILE, D_MODEL), F32)
    for c in range(D_FF // FF_CHUNK):
        ys = []
        for base in (0, D_FF):
            c0 = base + c * FF_CHUNK
            u = jnp.dot(hcat, wu_ref[:, c0:c0 + FF_CHUNK], preferred_element_type=F32)
            w = cw_ref[:, c0:c0 + FF_CHUNK]
            y = (cb_ref[:, c0:c0 + FF_CHUNK] + u[lo - 1:lo - 1 + TOK_TILE] * w[0:1]
                 + u[lo:lo + TOK_TILE] * w[1:2] + u[lo + 1:lo + 1 + TOK_TILE] * w[2:3])
            ys.append(y)
        act = (ys[0] * _sigmoid(ys[0]) * ys[1]).astype(BF16)
        acc += jnp.dot(act, wd_ref[c * FF_CHUNK:(c + 1) * FF_CHUNK, :], preferred_element_type=F32)
    o_ref[0] = x_ref[0] + gate2 * acc


def _conv_ffn(xm, h2, mod_l, w_up, conv_w, conv_b, w_down, n_batch, Lc, tile_off, n_tiles_total):
    B, T, D = xm.shape
    nt = T // TOK_TILE
    r = TOK_TILE // BF16_ROWS
    last = T // BF16_ROWS - 1
    full = lambda a: pl.BlockSpec(a.shape, lambda b, t: (0,) * a.ndim)
    tok = pl.BlockSpec((1, TOK_TILE, D), lambda b, t: (b, t, 0))
    halo_prev = pl.BlockSpec((1, BF16_ROWS, D), lambda b, t: (b, jnp.maximum(t * r - 1, 0), 0))
    halo_next = pl.BlockSpec((1, BF16_ROWS, D), lambda b, t: (b, jnp.minimum((t + 1) * r, last), 0))
    return pl.pallas_call(
        functools.partial(_ffn_kernel, n_batch=n_batch, n_ctx_tiles=Lc // TOK_TILE, tile_off=tile_off,
                          n_tiles_total=n_tiles_total),
        grid=(B, nt),
        in_specs=[tok, tok, halo_prev, halo_next, full(mod_l), full(w_up), full(conv_w), full(conv_b),
                  full(w_down)],
        out_specs=tok,
        out_shape=jax.ShapeDtypeStruct((B, T, D), F32),
        compiler_params=_cparams("parallel", "arbitrary"),
        name="conv_ffn",
    )(xm, h2, h2, h2, mod_l, w_up, conv_w, conv_b, w_down)


def _with_ones_column(v):
    pad = jnp.zeros(v.shape[:-1] + (LANES - v.shape[-1],), v.dtype).at[..., 0].set(1.0)
    return jnp.concatenate([v, pad], axis=-1)


def kernel(x, c, ctx, c_ctx, w_mod, b_mod, norm1_g, norm2_g, w_in, win_qnorm_g, win_knorm_g, win_sink,
           hg_lower, hg_onorm_g, diff_qnorm_g, diff_knorm_g, diff_lambda, diff_onorm_g, w_out,
           w_up, conv_w, conv_b, w_down):
    B, L, D = x.shape
    Lc = ctx.shape[1]
    depth = w_mod.shape[0]
    Lt = Lc + L
    assert D == D_MODEL and L % GRID_W == 0
    assert Lc % TOK_TILE == 0 and L % TOK_TILE == 0 and TOK_TILE % HG_BLOCK == 0 and TOK_TILE % ATT_BLOCK == 0
    assert B + 1 <= SUBLANES

    vecs = jnp.zeros((SUBLANES, D), F32).at[:B].set(c.astype(F32)).at[B].set(c_ctx.astype(F32))
    mod = _adaln(vecs, w_mod.astype(F32), b_mod.astype(F32))
    lower, lam_raw = _prep_params(hg_lower, diff_lambda)

    tabs = _rope_tables(L, Lc, WIN_HEAD_DIM) + _rope_tables(L, Lc, DIFF_QK_DIM)
    g64 = _block_diag_ones(256, 64)
    g32 = _block_diag_ones(256, 32)
    row = lambda v, reps: jnp.tile(v.astype(F32), reps).reshape(1, -1)

    xc = jnp.concatenate([ctx.astype(F32), x.astype(F32)], axis=1)
    n_tiles_total = Lt // TOK_TILE
    nct = Lc // TOK_TILE

    for l in range(depth):
        need_ctx = l < depth - 1
        lam_init = 0.8 - 0.6 * math.exp(-0.3 * l)
        gains = (row(win_qnorm_g[l], WIN_HEADS), row(win_knorm_g[l], WIN_KV_HEADS),
                 row(diff_qnorm_g[l], 2 * DIFF_HEADS), row(diff_knorm_g[l], 2 * DIFF_HEADS))
        (qa, ka, va, hqf, hlff, hkf, hqb, hlfb, hkb, hi, hgate, qd, kd, vd) = _inproj(
            xc, mod[l], norm1_g[l].reshape(1, D).astype(F32), w_in[l].astype(BF16), tabs, gains, lower[l],
            g64, g32, B, Lc)

        qa_h = qa.reshape(B, Lt, WIN_HEADS, WIN_HEAD_DIM).transpose(0, 2, 1, 3)
        kaT = ka.reshape(B, Lt, WIN_KV_HEADS, WIN_HEAD_DIM).transpose(0, 2, 3, 1)
        va_h = _with_ones_column(va.reshape(B, Lt, WIN_KV_HEADS, WIN_HEAD_DIM).transpose(0, 2, 1, 3))
        sink = win_sink[l].astype(F32) * LOG2E
        oa_lat, oa_ctx = _win_attention(sink, qa_h, kaT, va_h, L, Lc, need_ctx)

        of = _hgrn_scan(hqf, hlff, hkf, hi, g64, Lc, reverse=False)
        ob = _hgrn_scan(hqb, hlfb, hkb, hi, g64, Lc, reverse=True)

        qdT = qd.transpose(0, 2, 1)
        vdT = vd.reshape(B, Lt, DIFF_HEADS, DIFF_V_DIM).transpose(0, 2, 3, 1)
        fill = jnp.zeros((B, DIFF_HEADS, DIFF_VT_ROWS - DIFF_V_DIM, Lt), BF16).at[:, :, 0].set(1.0)
        vdT_aug = jnp.concatenate([vdT, fill], axis=2)
        lam = lam_raw[l, 0, :1] + lam_init
        dog = diff_onorm_g[l].reshape(DIFF_V_DIM, 1).astype(F32)
        od_lat = _diff_attention(lam, qdT, kd, vdT_aug, dog, Lc, L, Lt, 1.0 - lam_init)

        def token_major(o):
            return o.transpose(0, 2, 1, 3).reshape(B, o.shape[2], -1)

        def token_major_t(oT):
            return oT.transpose(0, 3, 1, 2).reshape(B, oT.shape[3], -1)

        if need_ctx:
            od_ctx = _diff_attention(lam, qdT, kd, vdT_aug, dog, 0, Lc, Lc, 1.0 - lam_init)
            oa = jnp.concatenate([token_major(oa_ctx), token_major(oa_lat)], axis=1)
            od = jnp.concatenate([token_major_t(od_ctx), token_major_t(od_lat)], axis=1)
            tile_off, mix_off = 0, 0
        else:
            oa, od = token_major(oa_lat), token_major_t(od_lat)
            tile_off, mix_off = nct, nct

        xm, h2 = _outproj(xc, oa, of, ob, hgate, od, mod[l], w_out[l].astype(BF16),
                          row(hg_onorm_g[l], HG_HEADS), norm2_g[l].reshape(1, D).astype(F32), g64,
                          B, Lc, tile_off, mix_off)
        xc = _conv_ffn(xm, h2, mod[l], w_up[l].astype(BF16), conv_w[l].astype(F32),
                       conv_b[l].reshape(1, -1).astype(F32), w_down[l].astype(BF16),
                       B, Lc, tile_off, n_tiles_total)

    return xc
```

```python
import functools
import math

import numpy as np
import jax
import jax.numpy as jnp
from jax import lax
from jax.experimental import pallas as pl
from jax.experimental.pallas import tpu as pltpu

F32 = jnp.float32
BF16 = jnp.bfloat16

D_MODEL = 1024
GRID_W = 64
WIN_HEADS, WIN_KV_HEADS, WIN_HEAD_DIM = 8, 2, 64
WINDOW = 128
HG_HEADS, HG_KEY_DIM, HG_VAL_DIM = 4, 64, 64
DIFF_HEADS, DIFF_QK_DIM, DIFF_V_DIM = 4, 32, 64
D_FF = 11 * D_MODEL // 4
ROPE_BASE = 10000.0
EPS = 1e-6
LOG2E = math.log2(math.e)

_SPLITS = (512, 128, 128, 256, 256, 256, 256, 256, 256, 256, 256, 256)
_OFF = np.concatenate([[0], np.cumsum(_SPLITS)]).tolist()
IN_WIDTH = _OFF[-1]
(C_AQ, C_AK, C_AV, C_BQF, C_BFF, C_BQB, C_BFB, C_BI, C_BG, C_CQ, C_CK, C_CV) = _OFF[:-1]

LANES = 128
SUBLANES = 8
BF16_ROWS = 16
VMEM_LIMIT = 48 * 1024 * 1024

TOK_TILE = 256
ATT_BLOCK = 128
DIFF_TQ = 256
DIFF_MAX_SHIFT = 60.0
DIFF_VT_ROWS = 80
HG_BLOCK = 256
HG_GROUP = 128
HG_SUB = 16
FF_CHUNK = 256
INPROJ_AHEAD = 512
NEG = -1e30


def _cparams(*sem):
    return pltpu.CompilerParams(dimension_semantics=sem, vmem_limit_bytes=VMEM_LIMIT)


def _sigmoid(x):
    return 1.0 / (1.0 + jnp.exp(-x))


def _split3(x):
    hi = x.astype(BF16)
    r = x - hi.astype(F32)
    mid = r.astype(BF16)
    lo = (r - mid.astype(F32)).astype(BF16)
    return hi, mid, lo


def _group_sumsq(x, g):
    sq = x * x
    hi = sq.astype(BF16)
    lo = (sq - hi.astype(F32)).astype(BF16)
    return jnp.dot(hi, g, preferred_element_type=F32) + jnp.dot(lo, g, preferred_element_type=F32)


def _block_diag_ones(width, group):
    i = np.arange(width)
    return jnp.asarray((i[:, None] // group) == (i[None, :] // group), dtype=BF16)


def _prep_kernel(hl_ref, dl_ref, lb_ref, lam_ref, *, depth):
    raw = [hl_ref[l] for l in range(depth)]
    mx = raw[0]
    for l in range(1, depth):
        mx = jnp.maximum(mx, raw[l])
    ex = [jnp.exp(r - mx) for r in raw]
    tot = ex[0]
    for l in range(1, depth):
        tot = tot + ex[l]
    p = [e / tot for e in ex]
    run = p[0]
    lb_ref[0] = run - p[0]
    for l in range(1, depth):
        run = run + p[l]
        lb_ref[l] = run - p[0]
    for l in range(depth):
        d = dl_ref[l]
        a = jnp.sum(d[0:1] * d[1:2], axis=-1, keepdims=True)
        c = jnp.sum(d[2:3] * d[3:4], axis=-1, keepdims=True)
        lam_ref[l] = jnp.broadcast_to(jnp.exp(a) - jnp.exp(c), (1, LANES))


def _prep_params(hg_lower, diff_lambda):
    depth = hg_lower.shape[0]
    return pl.pallas_call(
        functools.partial(_prep_kernel, depth=depth),
        out_shape=(jax.ShapeDtypeStruct(hg_lower.shape, F32),
                   jax.ShapeDtypeStruct((depth, 1, LANES), F32)),
        name="prep_params",
    )(hg_lower.astype(F32), diff_lambda.astype(F32))


def _adaln_kernel(v_ref, w_ref, b_ref, o_ref):
    v = v_ref[...]
    a = v * _sigmoid(v)
    a_hi, a_mid, _ = _split3(a)
    w = w_ref[0]
    w_hi, w_mid, _ = _split3(w)
    acc = jnp.dot(a_hi, w_hi, preferred_element_type=F32)
    acc += jnp.dot(a_mid, w_hi, preferred_element_type=F32)
    acc += jnp.dot(a_hi, w_mid, preferred_element_type=F32)
    o_ref[0] = acc + b_ref[0]


def _adaln(vecs, w_mod, b_mod):
    depth, d, n6 = w_mod.shape
    rows = vecs.shape[0]
    cb = 768
    assert n6 % cb == 0
    return pl.pallas_call(
        _adaln_kernel,
        grid=(depth, n6 // cb),
        in_specs=[pl.BlockSpec((rows, d), lambda l, j: (0, 0)),
                  pl.BlockSpec((1, d, cb), lambda l, j: (l, 0, j)),
                  pl.BlockSpec((1, 1, cb), lambda l, j: (l, 0, j))],
        out_specs=pl.BlockSpec((1, rows, cb), lambda l, j: (l, 0, j)),
        out_shape=jax.ShapeDtypeStruct((depth, rows, n6), F32),
        compiler_params=_cparams("arbitrary", "arbitrary"),
        name="adaln",
    )(vecs, w_mod, b_mod.reshape(depth, 1, n6))


def _rope_tables(L, Lc, dim):
    rows = L // GRID_W
    row = jnp.repeat(jnp.arange(rows, dtype=F32), GRID_W)
    col = jnp.tile(jnp.arange(GRID_W, dtype=F32), rows)
    axis_dim = dim // 2
    n = axis_dim // 2
    inv = jnp.power(ROPE_BASE, -jnp.arange(n, dtype=F32) * 2.0 / axis_dim)
    ar = row[:, None] * inv[None, :]
    ac = col[:, None] * inv[None, :]
    cos = jnp.concatenate([jnp.cos(ar), jnp.cos(ar), jnp.cos(ac), jnp.cos(ac)], axis=-1)
    sin = jnp.concatenate([-jnp.sin(ar), jnp.sin(ar), -jnp.sin(ac), jnp.sin(ac)], axis=-1)
    cos = jnp.concatenate([jnp.ones((Lc, dim), F32), cos], axis=0)
    sin = jnp.concatenate([jnp.zeros((Lc, dim), F32), sin], axis=0)
    rep = LANES // dim
    return jnp.tile(cos, (1, rep)), jnp.tile(sin, (1, rep))


def _rope(x, cos, sin, quarter):
    lane = lax.broadcasted_iota(jnp.int32, x.shape, 1)
    first = (lane % (2 * quarter)) < quarter
    partner = jnp.where(first, pltpu.roll(x, LANES - quarter, 1), pltpu.roll(x, quarter, 1))
    return x * cos + partner * sin


def _inproj_kernel(x_ref, mod_ref, n1g_ref, w_ref, cosa_ref, sina_ref, cosd_ref, sind_ref,
                   gqa_ref, gka_ref, gqd_ref, gkd_ref, lb_ref, g64_ref, g32_ref,
                   qa_ref, ka_ref, va_ref, hqf_ref, hlff_ref, hkf_ref, hqb_ref, hlfb_ref, hkb_ref,
                   hi_ref, hg_ref, qd_ref, kd_ref, vd_ref, p_ref, *, n_batch, n_ctx_tiles):
    b = pl.program_id(0)
    t = pl.program_id(1)
    mrow = jnp.where(t < n_ctx_tiles, n_batch, b)
    m = mod_ref[pl.ds(mrow, 1), :]
    shift, scale = m[:, 0:D_MODEL], m[:, D_MODEL:2 * D_MODEL]

    x = x_ref[0]
    ms = jnp.mean(x * x, axis=-1, keepdims=True)
    h = (x * lax.rsqrt(ms + EPS) * n1g_ref[...]) * (1.0 + scale) + shift
    h = h.astype(BF16)

    edges = sorted(set(_OFF) | {C_AQ + 256})
    issued = [0]

    def proj(c0, width):
        target = min(c0 + width + INPROJ_AHEAD, IN_WIDTH)
        while issued[0] < target:
            e0 = issued[0]
            e1 = min(e for e in edges if e > e0)
            p_ref[:, e0:e1] = jnp.dot(h, w_ref[:, e0:e1], preferred_element_type=F32)
            issued[0] = e1
        return p_ref[:, c0:c0 + width]

    g64 = g64_ref[...]
    g32 = g32_ref[...]
    cosa, sina = cosa_ref[...], sina_ref[...]
    cosd, sind = cosd_ref[...], sind_ref[...]

    for c in range(2):
        p = proj(C_AQ + 256 * c, 256)
        qn = p * lax.rsqrt(_group_sumsq(p, g64) * (1.0 / WIN_HEAD_DIM) + EPS)
        qn = qn * (gqa_ref[:, 256 * c:256 * (c + 1)] * (WIN_HEAD_DIM ** -0.5 * LOG2E))
        for j in range(2):
            r = _rope(qn[:, LANES * j:LANES * (j + 1)], cosa, sina, WIN_HEAD_DIM // 4)
            qa_ref[0, :, 256 * c + LANES * j:256 * c + LANES * (j + 1)] = r.astype(BF16)
    p = proj(C_AK, 128)
    kn = p * lax.rsqrt(_group_sumsq(p, g64[:LANES, :LANES]) * (1.0 / WIN_HEAD_DIM) + EPS) * gka_ref[...]
    ka_ref[0] = _rope(kn, cosa, sina, WIN_HEAD_DIM // 4).astype(BF16)
    va_ref[0] = proj(C_AV, 128).astype(BF16)

    for d, (cq, cf, q_ref, lf_ref, k_ref) in enumerate(((C_BQF, C_BFF, hqf_ref, hlff_ref, hkf_ref),
                                                        (C_BQB, C_BFB, hqb_ref, hlfb_ref, hkb_ref))):
        q_ref[0] = proj(cq, 256)
        lb = lb_ref[d:d + 1, :]
        f = lb + (1.0 - lb) * _sigmoid(proj(cf, 256))
        lf_ref[0] = jnp.log(f)
        k_ref[0] = 1.0 - f
    hi_ref[0] = proj(C_BI, 256).astype(BF16)
    g = proj(C_BG, 256)
    hg_ref[0] = g * _sigmoid(g)

    for c0, gain_ref, o_ref, mult in ((C_CQ, gqd_ref, qd_ref, DIFF_QK_DIM ** -0.5 * LOG2E),
                                      (C_CK, gkd_ref, kd_ref, 1.0)):
        p = proj(c0, 256)
        pn = p * lax.rsqrt(_group_sumsq(p, g32) * (1.0 / DIFF_QK_DIM) + EPS) * (gain_ref[...] * mult)
        for j in range(2):
            r = _rope(pn[:, LANES * j:LANES * (j + 1)], cosd, sind, DIFF_QK_DIM // 4)
            o_ref[0, :, LANES * j:LANES * (j + 1)] = r.astype(BF16)
    vd_ref[0] = proj(C_CV, 256).astype(BF16)


def _inproj(xc, mod_l, n1g, w_in, tabs, gains, lb_l, g64, g32, n_batch, Lc):
    B, Lt, D = xc.shape
    nt = Lt // TOK_TILE
    cosa, sina, cosd, sind = tabs
    gqa, gka, gqd, gkd = gains
    full = lambda a: pl.BlockSpec(a.shape, lambda b, t: (0,) * a.ndim)
    tab = pl.BlockSpec((TOK_TILE, LANES), lambda b, t: (t, 0))
    tok = lambda w: pl.BlockSpec((1, TOK_TILE, w), lambda b, t: (b, t, 0))
    widths = (512, 128, 128, 256, 256, 256, 256, 256, 256, 256, 256, 256, 256, 256)
    dtypes = (BF16, BF16, BF16, F32, F32, F32, F32, F32, F32, BF16, F32, BF16, BF16, BF16)
    return pl.pallas_call(
        functools.partial(_inproj_kernel, n_batch=n_batch, n_ctx_tiles=Lc // TOK_TILE),
        grid=(B, nt),
        in_specs=[tok(D), full(mod_l), full(n1g), full(w_in), tab, tab, tab, tab,
                  full(gqa), full(gka), full(gqd), full(gkd), full(lb_l), full(g64), full(g32)],
        out_specs=[tok(w) for w in widths],
        out_shape=[jax.ShapeDtypeStruct((B, Lt, w), dt) for w, dt in zip(widths, dtypes)],
        scratch_shapes=[pltpu.VMEM((TOK_TILE, IN_WIDTH), F32)],
        compiler_params=_cparams("parallel", "arbitrary"),
        name="inproj",
    )(xc, mod_l, n1g, w_in, cosa, sina, cosd, sind, gqa, gka, gqd, gkd, lb_l, g64, g32)


def _gqa_softmax_pv(s_blocks, v_blocks, sinks):
    rows = s_blocks[0].shape[0]
    per_head = rows // len(sinks)
    rid = lax.broadcasted_iota(jnp.int32, (rows, 1), 0)
    sink = jnp.full((rows, 1), sinks[-1], F32)
    for g in range(len(sinks) - 2, -1, -1):
        sink = jnp.where(rid < (g + 1) * per_head, sinks[g], sink)
    s = jnp.concatenate(s_blocks, axis=1)
    m = jnp.maximum(jnp.max(s, axis=-1, keepdims=True), sink)
    e = jnp.exp2(s - m).astype(BF16)
    pv = jnp.dot(e, jnp.concatenate(v_blocks, axis=0), preferred_element_type=F32)
    den = pv[:, WIN_HEAD_DIM:WIN_HEAD_DIM + 1] + jnp.exp2(sink - m)
    return pv[:, :WIN_HEAD_DIM] / den


def _win_attn_kernel(sink_ref, q_ref, kp_ref, kc_ref, kn_ref, kx_ref, vp_ref, vc_ref, vn_ref, vx_ref, o_ref):
    j = pl.program_id(1)
    nb = pl.num_programs(1)
    group = WIN_HEADS // WIN_KV_HEADS
    rows = group * ATT_BLOCK
    t = lax.broadcasted_iota(jnp.int32, (rows, ATT_BLOCK), 0) % ATT_BLOCK
    s = lax.broadcasted_iota(jnp.int32, (rows, ATT_BLOCK), 1)
    mask_prev = (s >= t) & (j > 0)
    mask_next = (s <= t) & (j < nb - 1)
    def scores(hk):
        q = q_ref[0, hk * group:(hk + 1) * group].reshape(rows, WIN_HEAD_DIM)
        dot = lambda k_ref: jnp.dot(q, k_ref[0, hk], preferred_element_type=F32)
        return [jnp.where(mask_prev, dot(kp_ref), NEG), dot(kc_ref),
                jnp.where(mask_next, dot(kn_ref), NEG), dot(kx_ref)]

    blocks_all = [scores(hk) for hk in range(WIN_KV_HEADS)]
    for hk in range(WIN_KV_HEADS):
        o = _gqa_softmax_pv(blocks_all[hk], [vp_ref[0, hk], vc_ref[0, hk], vn_ref[0, hk], vx_ref[0, hk]],
                            [sink_ref[hk * group + g] for g in range(group)])
        o_ref[0, hk * group:(hk + 1) * group] = o.reshape(group, ATT_BLOCK, WIN_HEAD_DIM).astype(BF16)


def _ctx_attn_kernel(sink_ref, q_ref, kx_ref, vx_ref, o_ref):
    group = WIN_HEADS // WIN_KV_HEADS
    n = q_ref.shape[2]
    for hk in range(WIN_KV_HEADS):
        q = q_ref[0, hk * group:(hk + 1) * group].reshape(group * n, WIN_HEAD_DIM)
        s_ctx = jnp.dot(q, kx_ref[0, hk], preferred_element_type=F32)
        o = _gqa_softmax_pv([s_ctx], [vx_ref[0, hk]], [sink_ref[hk * group + g] for g in range(group)])
        o_ref[0, hk * group:(hk + 1) * group] = o.reshape(group, n, WIN_HEAD_DIM).astype(BF16)


def _win_attention(sink, q_h, kT, v_h, L, Lc, need_ctx):
    B = q_h.shape[0]
    nb = L // ATT_BLOCK
    off = Lc // ATT_BLOCK
    dh = WIN_HEAD_DIM
    smem = pl.BlockSpec(memory_space=pltpu.SMEM)
    kblk = lambda f: pl.BlockSpec((1, WIN_KV_HEADS, dh, ATT_BLOCK), lambda b, j: (b, 0, 0, off + f(j)))
    vblk = lambda f: pl.BlockSpec((1, WIN_KV_HEADS, ATT_BLOCK, LANES), lambda b, j: (b, 0, off + f(j), 0))
    prev = lambda j: jnp.maximum(j - 1, 0)
    cur = lambda j: j
    nxt = lambda j: jnp.minimum(j + 1, nb - 1)
    kctx = pl.BlockSpec((1, WIN_KV_HEADS, dh, Lc), lambda b, j: (b, 0, 0, 0))
    vctx = pl.BlockSpec((1, WIN_KV_HEADS, Lc, LANES), lambda b, j: (b, 0, 0, 0))
    o_lat = pl.pallas_call(
        _win_attn_kernel,
        grid=(B, nb),
        in_specs=[smem, pl.BlockSpec((1, WIN_HEADS, ATT_BLOCK, dh), lambda b, j: (b, 0, off + j, 0)),
                  kblk(prev), kblk(cur), kblk(nxt), kctx, vblk(prev), vblk(cur), vblk(nxt), vctx],
        out_specs=pl.BlockSpec((1, WIN_HEADS, ATT_BLOCK, dh), lambda b, j: (b, 0, j, 0)),
        out_shape=jax.ShapeDtypeStruct((B, WIN_HEADS, L, dh), BF16),
        compiler_params=_cparams("parallel", "arbitrary"),
        name="win_attn",
    )(sink, q_h, kT, kT, kT, kT, v_h, v_h, v_h, v_h)
    if not need_ctx:
        return o_lat, None
    o_ctx = pl.pallas_call(
        _ctx_attn_kernel,
        grid=(B,),
        in_specs=[smem, pl.BlockSpec((1, WIN_HEADS, Lc, dh), lambda b: (b, 0, 0, 0)),
                  pl.BlockSpec((1, WIN_KV_HEADS, dh, Lc), lambda b: (b, 0, 0, 0)),
                  pl.BlockSpec((1, WIN_KV_HEADS, Lc, LANES), lambda b: (b, 0, 0, 0))],
        out_specs=pl.BlockSpec((1, WIN_HEADS, Lc, dh), lambda b: (b, 0, 0, 0)),
        out_shape=jax.ShapeDtypeStruct((B, WIN_HEADS, Lc, dh), BF16),
        compiler_params=_cparams("parallel"),
        name="ctx_attn",
    )(sink, q_h, kT, v_h)
    return o_lat, o_ctx


def _hgrn_kernel(q_ref, lf_ref, k_ref, v_ref, g64_ref, o_ref, st_ref, *, reverse):
    @pl.when(pl.program_id(1) == 0)
    def _():
        st_ref[...] = jnp.zeros_like(st_ref)

    width = HG_HEADS * HG_KEY_DIM
    n_pairs = width // LANES
    n_groups = HG_BLOCK // HG_GROUP
    n_sub = HG_GROUP // HG_SUB
    half = HG_SUB // 2
    ri = lax.broadcasted_iota(jnp.int32, (HG_GROUP, HG_GROUP), 0)
    ci = lax.broadcasted_iota(jnp.int32, (HG_GROUP, HG_GROUP), 1)
    in_sub = (ri // HG_SUB) == (ci // HG_SUB)
    tri = (in_sub & ((ci >= ri) if reverse else (ci <= ri))).astype(BF16)
    rows_full = lax.broadcasted_iota(jnp.int32, (HG_SUB, width), 0)
    rows_half = lax.broadcasted_iota(jnp.int32, (half, width), 0) + (0 if reverse else half)
    same_head = (lax.broadcasted_iota(jnp.int32, (LANES, LANES), 0) // HG_VAL_DIM
                 == lax.broadcasted_iota(jnp.int32, (LANES, LANES), 1) // HG_KEY_DIM)
    g64 = g64_ref[...]
    end_row = 0 if reverse else HG_SUB - 1
    half_lo = 0 if reverse else half

    def is_full(s):
        return (s >= half) if reverse else (s < half)

    def group(gi, carry):
        g = (n_groups - 1 - gi) if reverse else gi
        g0 = pl.multiple_of(g * HG_GROUP, HG_GROUP)
        lf = lf_ref[0, pl.ds(g0, HG_GROUP), :]
        q = q_ref[0, pl.ds(g0, HG_GROUP), :]
        k = k_ref[0, pl.ds(g0, HG_GROUP), :]
        v = v_ref[0, pl.ds(g0, HG_GROUP), :]
        vf = v.astype(F32)

        hi, mid, lo = _split3(lf)
        bcum = (jnp.dot(tri, hi, preferred_element_type=F32) + jnp.dot(tri, mid, preferred_element_type=F32)
                + jnp.dot(tri, lo, preferred_element_type=F32))
        qt = (q * jnp.exp(bcum)).astype(BF16)

        sts = [st_ref[p] for p in range(n_pairs)]
        inter = [None] * n_sub
        for c in (range(n_sub - 1, -1, -1) if reverse else range(n_sub)):
            r0 = c * HG_SUB
            bc = bcum[r0:r0 + HG_SUB]
            b_end = bc[end_row:end_row + 1]
            khat = (k[r0:r0 + HG_SUB] * jnp.exp(b_end - bc)).astype(BF16)
            dec = jnp.exp(b_end)
            o_parts = []
            for p in range(n_pairs):
                ls = slice(p * LANES, (p + 1) * LANES)
                st_b = jnp.where(same_head, sts[p], 0.0).astype(BF16)
                o_parts.append(lax.dot_general(qt[r0:r0 + HG_SUB, ls], st_b, (((1,), (1,)), ((), ())),
                                               preferred_element_type=F32))
                upd = lax.dot_general(v[r0:r0 + HG_SUB, ls], khat[:, ls], (((0,), (0,)), ((), ())),
                                      preferred_element_type=F32)
                sts[p] = sts[p] * dec[:, ls] + upd
            inter[c] = jnp.concatenate(o_parts, axis=1)
        for p in range(n_pairs):
            st_ref[p] = sts[p]

        parts = []
        for c in range(n_sub):
            r0 = c * HG_SUB
            bc, qc, kc = bcum[r0:r0 + HG_SUB], q[r0:r0 + HG_SUB], k[r0:r0 + HG_SUB]
            for s in range(HG_SUB):
                if is_full(s):
                    rows, bt, qq = rows_full, bc, qc
                else:
                    rows, bt, qq = rows_half, bc[half_lo:half_lo + half], qc[half_lo:half_lo + half]
                valid = (rows <= s) if reverse else (rows >= s)
                parts.append(jnp.where(valid, qq * jnp.exp(bt - bc[s:s + 1]) * kc[s:s + 1], 0.0))
        a = jnp.dot(jnp.concatenate(parts, axis=0).astype(BF16), g64, preferred_element_type=F32)
        intra = []
        off = 0
        for c in range(n_sub):
            r0 = c * HG_SUB
            o_full = jnp.zeros((HG_SUB, width), F32)
            o_half = jnp.zeros((half, width), F32)
            for s in range(HG_SUB):
                vs = vf[r0 + s:r0 + s + 1]
                if is_full(s):
                    o_full = o_full + a[off:off + HG_SUB] * vs
                    off += HG_SUB
                else:
                    o_half = o_half + a[off:off + half] * vs
                    off += half
            pad = jnp.zeros((half, width), F32)
            intra.append(o_full + jnp.concatenate([o_half, pad] if reverse else [pad, o_half], axis=0))

        for c in range(n_sub):
            o_ref[0, pl.ds(g0 + c * HG_SUB, HG_SUB), :] = intra[c] + inter[c]
        return carry

    lax.fori_loop(0, n_groups, group, 0)


def _hgrn_scan(q, lf, k, v, g64, Lc, reverse):
    B, Lt, W = q.shape
    nblk = Lt // HG_BLOCK
    nctx = Lc // HG_BLOCK
    if reverse:
        order = lambda i: jnp.where(i < nctx, nctx - 1 - i, nctx + (nblk - 1 - i))
    else:
        order = lambda i: i
    spec = pl.BlockSpec((1, HG_BLOCK, W), lambda b, i: (b, order(i), 0))
    return pl.pallas_call(
        functools.partial(_hgrn_kernel, reverse=reverse),
        grid=(B, nblk),
        in_specs=[spec, spec, spec, spec, pl.BlockSpec(g64.shape, lambda b, i: (0, 0))],
        out_specs=spec,
        out_shape=jax.ShapeDtypeStruct((B, Lt, W), F32),
        scratch_shapes=[pltpu.VMEM((W // LANES, LANES, LANES), F32)],
        compiler_params=_cparams("parallel", "arbitrary"),
        name="hgrn_bwd" if reverse else "hgrn_fwd",
    )(q, lf, k, v, g64)


def _diff_attn_kernel(lam_ref, qT_ref, k_ref, vT_ref, kn_ref, og_ref, o_ref, *, key_block, n_keys, post_scale):
    nkb = n_keys // key_block
    n_chain = 2 * DIFF_HEADS
    per_half = LANES // DIFF_QK_DIM
    band = lax.broadcasted_iota(jnp.int32, (LANES, DIFF_TQ), 0) // DIFF_QK_DIM
    ws = []
    for j in range(n_chain):
        half = qT_ref[0, (j // per_half) * LANES:(j // per_half + 1) * LANES, :]
        ws.append(jnp.where(band == j % per_half, half, jnp.zeros_like(half)))

    def scores(kb):
        k0 = pl.multiple_of(kb * key_block, key_block)
        kblks = [k_ref[0, pl.ds(k0, key_block), p * LANES:(p + 1) * LANES] for p in range(n_chain // per_half)]
        sTs = [jnp.dot(kblks[j // per_half], ws[j], preferred_element_type=F32) for j in range(n_chain)]
        vTs = [vT_ref[0, h, :, pl.ds(k0, key_block)] for h in range(DIFF_HEADS)]
        return sTs, vTs

    def pv(vTs, eTs):
        return [jnp.dot(vTs[j // 2], eTs[j], preferred_element_type=F32) for j in range(n_chain)]

    qf = qT_ref[0].astype(F32)
    qn = jnp.sqrt(jnp.sum((qf * qf).reshape(n_chain, DIFF_QK_DIM, DIFF_TQ), axis=1))
    bound = qn * kn_ref[0] * (1.0 + 2.0 ** -10) + 2.0 ** -10
    shift = [bound[j:j + 1] for j in range(n_chain)]
    acc0 = tuple(jnp.zeros((DIFF_VT_ROWS, DIFF_TQ), F32) for _ in range(n_chain))

    def fixed_shift():
        def body(kb, accs):
            sTs, vTs = scores(kb)
            pvs = pv(vTs, [jnp.exp2(sTs[j] - shift[j]).astype(BF16) for j in range(n_chain)])
            return tuple(accs[j] + pvs[j] for j in range(n_chain))
        return lax.fori_loop(0, nkb, body, acc0)

    def running_max():
        def body(kb, carry):
            ms, accs = carry
            sTs, vTs = scores(kb)
            m_new = [jnp.maximum(ms[j], jnp.max(sTs[j], axis=0, keepdims=True)) for j in range(n_chain)]
            pvs = pv(vTs, [jnp.exp2(sTs[j] - m_new[j]).astype(BF16) for j in range(n_chain)])
            accs = tuple(accs[j] * jnp.exp2(ms[j] - m_new[j]) + pvs[j] for j in range(n_chain))
            return tuple(m_new), accs
        m0 = tuple(jnp.full((1, DIFF_TQ), NEG, F32) for _ in range(n_chain))
        return lax.fori_loop(0, nkb, body, (m0, acc0))[1]

    accs = lax.cond(jnp.max(bound) <= DIFF_MAX_SHIFT, fixed_shift, running_max)
    lam = lam_ref[0]
    for h in range(DIFF_HEADS):
        a0, a1 = accs[2 * h], accs[2 * h + 1]
        o = (a0[:DIFF_V_DIM] / a0[DIFF_V_DIM:DIFF_V_DIM + 1]
             - lam * (a1[:DIFF_V_DIM] / a1[DIFF_V_DIM:DIFF_V_DIM + 1]))
        ms = jnp.mean(o * o, axis=0, keepdims=True)
        o_ref[0, h] = (o * lax.rsqrt(ms + EPS) * (og_ref[...] * post_scale)).astype(BF16)


def _pick_key_block(n):
    best = LANES
    for kb in range(LANES, min(n, 1024) + 1, LANES):
        if n % kb == 0:
            best = kb
    return best


def _diff_attention(lam, qT, k, vT_aug, k_norm, og_col, q_off, n_q, n_keys, post_scale):
    B = qT.shape[0]
    width = 2 * DIFF_HEADS * DIFF_QK_DIM
    nq = n_q // DIFF_TQ
    qo = q_off // DIFF_TQ
    n_chain = 2 * DIFF_HEADS
    return pl.pallas_call(
        functools.partial(_diff_attn_kernel, key_block=_pick_key_block(n_keys), n_keys=n_keys,
                          post_scale=post_scale),
        grid=(B, nq),
        in_specs=[pl.BlockSpec(memory_space=pltpu.SMEM),
                  pl.BlockSpec((1, width, DIFF_TQ), lambda b, i: (b, 0, qo + i)),
                  pl.BlockSpec((1, n_keys, width), lambda b, i: (b, 0, 0)),
                  pl.BlockSpec((1, DIFF_HEADS, DIFF_VT_ROWS, n_keys), lambda b, i: (b, 0, 0, 0)),
                  pl.BlockSpec((1, n_chain, 1), lambda b, i: (b, 0, 0)),
                  pl.BlockSpec((DIFF_V_DIM, 1), lambda b, i: (0, 0))],
        out_specs=pl.BlockSpec((1, DIFF_HEADS, DIFF_V_DIM, DIFF_TQ), lambda b, i: (b, 0, 0, i)),
        out_shape=jax.ShapeDtypeStruct((B, DIFF_HEADS, DIFF_V_DIM, n_q), BF16),
        compiler_params=_cparams("parallel", "arbitrary"),
        name="diff_attn",
    )(lam, qT, k, vT_aug, k_norm, og_col)


def _outproj_kernel(x_ref, oa_ref, of_ref, ob_ref, gate_ref, od_ref, mod_ref, w_ref, hog_ref, n2g_ref, g64_ref,
                    xm_ref, h2_ref, *, n_batch, n_ctx_tiles, tile_off):
    b = pl.program_id(0)
    t = pl.program_id(1) + tile_off
    mrow = jnp.where(t < n_ctx_tiles, n_batch, b)
    m = mod_ref[pl.ds(mrow, 1), :]
    gate1 = m[:, 2 * D_MODEL:3 * D_MODEL]
    shift2, scale2 = m[:, 3 * D_MODEL:4 * D_MODEL], m[:, 4 * D_MODEL:5 * D_MODEL]

    o = of_ref[0] + ob_ref[0]
    ss = _group_sumsq(o, g64_ref[...])
    wa, wb = WIN_HEADS * WIN_HEAD_DIM, HG_HEADS * HG_VAL_DIM
    mix = jnp.dot(oa_ref[0], w_ref[0:wa, :], preferred_element_type=F32)
    mix += jnp.dot(od_ref[0], w_ref[wa + wb:, :], preferred_element_type=F32)
    on = o * lax.rsqrt(ss * (1.0 / HG_VAL_DIM) + EPS) * hog_ref[...]
    ob = (on * gate_ref[0]).astype(BF16)
    mix += jnp.dot(ob, w_ref[wa:wa + wb, :], preferred_element_type=F32)

    x = x_ref[0] + gate1 * mix
    xm_ref[0] = x
    ms = jnp.mean(x * x, axis=-1, keepdims=True)
    h = (x * lax.rsqrt(ms + EPS) * n2g_ref[...]) * (1.0 + scale2) + shift2
    h2_ref[0] = h.astype(BF16)


def _outproj(xc, oa, of, ob, gate, od, mod_l, w_out, hog, n2g, g64, n_batch, Lc, tile_off, mix_off):
    B, Lt, D = xc.shape
    nt = Lt // TOK_TILE - tile_off
    full = lambda a: pl.BlockSpec(a.shape, lambda b, t: (0,) * a.ndim)
    comb = lambda w: pl.BlockSpec((1, TOK_TILE, w), lambda b, t: (b, t + tile_off, 0))
    part = lambda w: pl.BlockSpec((1, TOK_TILE, w), lambda b, t: (b, t + tile_off - mix_off, 0))
    outs = pl.BlockSpec((1, TOK_TILE, D), lambda b, t: (b, t, 0))
    return pl.pallas_call(
        functools.partial(_outproj_kernel, n_batch=n_batch, n_ctx_tiles=Lc // TOK_TILE, tile_off=tile_off),
        grid=(B, nt),
        in_specs=[comb(D), part(oa.shape[-1]), comb(256), comb(256), comb(256), part(od.shape[-1]),
                  full(mod_l), full(w_out), full(hog), full(n2g), full(g64)],
        out_specs=[outs, outs],
        out_shape=[jax.ShapeDtypeStruct((B, nt * TOK_TILE, D), F32),
                   jax.ShapeDtypeStruct((B, nt * TOK_TILE, D), BF16)],
        compiler_params=_cparams("parallel", "arbitrary"),
        name="outproj",
    )(xc, oa, of, ob, gate, od, mod_l, w_out, hog, n2g, g64)


def _ffn_kernel(x_ref, h_ref, hp_ref, hn_ref, mod_ref, wu_ref, cw_ref, cb_ref, wd_ref, o_ref, u_ref,
                *, n_batch, n_ctx_tiles, tile_off, n_tiles_total):
    b = pl.program_id(0)
    t = pl.program_id(1) + tile_off
    mrow = jnp.where(t < n_ctx_tiles, n_batch, b)
    gate2 = mod_ref[pl.ds(mrow, 1), :][:, 5 * D_MODEL:6 * D_MODEL]

    prev_ok = jnp.logical_and(t != 0, t != n_ctx_tiles)
    next_ok = jnp.logical_and(t != n_ctx_tiles - 1, t != n_tiles_total - 1)
    hp = jnp.where(prev_ok, hp_ref[0], jnp.zeros_like(hp_ref[0]))
    hn = jnp.where(next_ok, hn_ref[0], jnp.zeros_like(hn_ref[0]))
    hcat = jnp.concatenate([hp, h_ref[0], hn], axis=0)

    lo = BF16_ROWS
    n_chunks = D_FF // FF_CHUNK

    def up_project(c):
        for base in (0, D_FF):
            cols = slice(base + c * FF_CHUNK, base + (c + 1) * FF_CHUNK)
            u_ref[:, cols] = jnp.dot(hcat, wu_ref[:, cols], preferred_element_type=F32)

    def conv(c, base):
        cols = slice(base + c * FF_CHUNK, base + (c + 1) * FF_CHUNK)
        w = cw_ref[:, cols]
        return (cb_ref[:, cols] + u_ref[lo - 1:lo - 1 + TOK_TILE, cols] * w[0:1]
                + u_ref[lo:lo + TOK_TILE, cols] * w[1:2] + u_ref[lo + 1:lo + 1 + TOK_TILE, cols] * w[2:3])

    acc = jnp.zeros((TOK_TILE, D_MODEL), F32)
    ahead = 2
    for c in range(min(ahead, n_chunks)):
        up_project(c)
    for c in range(n_chunks):
        if c + ahead < n_chunks:
            up_project(c + ahead)
        a, val = conv(c, 0), conv(c, D_FF)
        act = (a * _sigmoid(a) * val).astype(BF16)
        acc += jnp.dot(act, wd_ref[c * FF_CHUNK:(c + 1) * FF_CHUNK, :], preferred_element_type=F32)
    o_ref[0] = x_ref[0] + gate2 * acc


def _conv_ffn(xm, h2, mod_l, w_up, conv_w, conv_b, w_down, n_batch, Lc, tile_off, n_tiles_total):
    B, T, D = xm.shape
    nt = T // TOK_TILE
    r = TOK_TILE // BF16_ROWS
    last = T // BF16_ROWS - 1
    full = lambda a: pl.BlockSpec(a.shape, lambda b, t: (0,) * a.ndim)
    tok = pl.BlockSpec((1, TOK_TILE, D), lambda b, t: (b, t, 0))
    halo_prev = pl.BlockSpec((1, BF16_ROWS, D), lambda b, t: (b, jnp.maximum(t * r - 1, 0), 0))
    halo_next = pl.BlockSpec((1, BF16_ROWS, D), lambda b, t: (b, jnp.minimum((t + 1) * r, last), 0))
    return pl.pallas_call(
        functools.partial(_ffn_kernel, n_batch=n_batch, n_ctx_tiles=Lc // TOK_TILE, tile_off=tile_off,
                          n_tiles_total=n_tiles_total),
        grid=(B, nt),
        in_specs=[tok, tok, halo_prev, halo_next, full(mod_l), full(w_up), full(conv_w), full(conv_b),
                  full(w_down)],
        out_specs=tok,
        out_shape=jax.ShapeDtypeStruct((B, T, D), F32),
        scratch_shapes=[pltpu.VMEM((TOK_TILE + 2 * BF16_ROWS, 2 * D_FF), F32)],
        compiler_params=_cparams("parallel", "arbitrary"),
        name="conv_ffn",
    )(xm, h2, h2, h2, mod_l, w_up, conv_w, conv_b, w_down)


def _with_ones_column(v):
    pad = jnp.zeros(v.shape[:-1] + (LANES - v.shape[-1],), v.dtype).at[..., 0].set(1.0)
    return jnp.concatenate([v, pad], axis=-1)


def kernel(x, c, ctx, c_ctx, w_mod, b_mod, norm1_g, norm2_g, w_in, win_qnorm_g, win_knorm_g, win_sink,
           hg_lower, hg_onorm_g, diff_qnorm_g, diff_knorm_g, diff_lambda, diff_onorm_g, w_out,
           w_up, conv_w, conv_b, w_down):
    B, L, D = x.shape
    Lc = ctx.shape[1]
    depth = w_mod.shape[0]
    Lt = Lc + L
    assert D == D_MODEL and L % GRID_W == 0
    assert Lc % TOK_TILE == 0 and L % TOK_TILE == 0 and TOK_TILE % HG_BLOCK == 0 and TOK_TILE % ATT_BLOCK == 0
    assert B + 1 <= SUBLANES

    vecs = jnp.zeros((SUBLANES, D), F32).at[:B].set(c.astype(F32)).at[B].set(c_ctx.astype(F32))
    mod = _adaln(vecs, w_mod.astype(F32), b_mod.astype(F32))
    lower, lam_raw = _prep_params(hg_lower, diff_lambda)

    tabs = _rope_tables(L, Lc, WIN_HEAD_DIM) + _rope_tables(L, Lc, DIFF_QK_DIM)
    g64 = _block_diag_ones(256, 64)
    g32 = _block_diag_ones(256, 32)
    row = lambda v, reps: jnp.tile(v.astype(F32), reps).reshape(1, -1)

    xc = jnp.concatenate([ctx.astype(F32), x.astype(F32)], axis=1)
    n_tiles_total = Lt // TOK_TILE
    nct = Lc // TOK_TILE

    for l in range(depth):
        need_ctx = l < depth - 1
        lam_init = 0.8 - 0.6 * math.exp(-0.3 * l)
        gains = (row(win_qnorm_g[l], WIN_HEADS), row(win_knorm_g[l], WIN_KV_HEADS),
                 row(diff_qnorm_g[l], 2 * DIFF_HEADS), row(diff_knorm_g[l], 2 * DIFF_HEADS))
        (qa, ka, va, hqf, hlff, hkf, hqb, hlfb, hkb, hi, hgate, qd, kd, vd) = _inproj(
            xc, mod[l], norm1_g[l].reshape(1, D).astype(F32), w_in[l].astype(BF16), tabs, gains, lower[l],
            g64, g32, B, Lc)

        qa_h = qa.reshape(B, Lt, WIN_HEADS, WIN_HEAD_DIM).transpose(0, 2, 1, 3)
        kaT = ka.reshape(B, Lt, WIN_KV_HEADS, WIN_HEAD_DIM).transpose(0, 2, 3, 1)
        va_h = _with_ones_column(va.reshape(B, Lt, WIN_KV_HEADS, WIN_HEAD_DIM).transpose(0, 2, 1, 3))
        sink = win_sink[l].astype(F32) * LOG2E
        oa_lat, oa_ctx = _win_attention(sink, qa_h, kaT, va_h, L, Lc, need_ctx)

        of = _hgrn_scan(hqf, hlff, hkf, hi, g64, Lc, reverse=False)
        ob = _hgrn_scan(hqb, hlfb, hkb, hi, g64, Lc, reverse=True)

        qdT = qd.transpose(0, 2, 1)
        vdT = vd.reshape(B, Lt, DIFF_HEADS, DIFF_V_DIM).transpose(0, 2, 3, 1)
        fill = jnp.zeros((B, DIFF_HEADS, DIFF_VT_ROWS - DIFF_V_DIM, Lt), BF16).at[:, :, 0].set(1.0)
        vdT_aug = jnp.concatenate([vdT, fill], axis=2)
        lam = lam_raw[l, 0, :1] + lam_init
        dog = diff_onorm_g[l].reshape(DIFF_V_DIM, 1).astype(F32)
        kf = kd.astype(F32).reshape(B, Lt, 2 * DIFF_HEADS, DIFF_QK_DIM)
        k_norm = jnp.sqrt(jnp.max(jnp.sum(kf * kf, axis=-1), axis=1)).reshape(B, 2 * DIFF_HEADS, 1)
        od_lat = _diff_attention(lam, qdT, kd, vdT_aug, k_norm, dog, Lc, L, Lt, 1.0 - lam_init)

        def token_major(o):
            return o.transpose(0, 2, 1, 3).reshape(B, o.shape[2], -1)

        def token_major_t(oT):
            return oT.transpose(0, 3, 1, 2).reshape(B, oT.shape[3], -1)

        if need_ctx:
            od_ctx = _diff_attention(lam, qdT, kd, vdT_aug, k_norm, dog, 0, Lc, Lc, 1.0 - lam_init)
            oa = jnp.concatenate([token_major(oa_ctx), token_major(oa_lat)], axis=1)
            od = jnp.concatenate([token_major_t(od_ctx), token_major_t(od_lat)], axis=1)
            tile_off, mix_off = 0, 0
        else:
            oa, od = token_major(oa_lat), token_major_t(od_lat)
            tile_off, mix_off = nct, nct

        xm, h2 = _outproj(xc, oa, of, ob, hgate, od, mod[l], w_out[l].astype(BF16),
                          row(hg_onorm_g[l], HG_HEADS), norm2_g[l].reshape(1, D).astype(F32), g64,
                          B, Lc, tile_off, mix_off)
        xc = _conv_ffn(xm, h2, mod[l], w_up[l].astype(BF16), conv_w[l].astype(F32),
                       conv_b[l].reshape(1, -1).astype(F32), w_down[l].astype(BF16),
                       B, Lc, tile_off, n_tiles_total)

    return xc
```

```python
import functools
import math

import numpy as np
import jax
import jax.numpy as jnp
from jax import lax
from jax.experimental import pallas as pl
from jax.experimental.pallas import tpu as pltpu

F32 = jnp.float32
BF16 = jnp.bfloat16

D_MODEL = 1024
GRID_W = 64
WIN_HEADS, WIN_KV_HEADS, WIN_HEAD_DIM = 8, 2, 64
WINDOW = 128
HG_HEADS, HG_KEY_DIM, HG_VAL_DIM = 4, 64, 64
DIFF_HEADS, DIFF_QK_DIM, DIFF_V_DIM = 4, 32, 64
D_FF = 11 * D_MODEL // 4
ROPE_BASE = 10000.0
EPS = 1e-6
LOG2E = math.log2(math.e)

_SPLITS = (512, 128, 128, 256, 256, 256, 256, 256, 256, 256, 256, 256)
_OFF = np.concatenate([[0], np.cumsum(_SPLITS)]).tolist()
IN_WIDTH = _OFF[-1]
(C_AQ, C_AK, C_AV, C_BQF, C_BFF, C_BQB, C_BFB, C_BI, C_BG, C_CQ, C_CK, C_CV) = _OFF[:-1]

LANES = 128
SUBLANES = 8
BF16_ROWS = 16
VMEM_LIMIT = 48 * 1024 * 1024

TOK_TILE = 256
ATT_BLOCK = 128
DIFF_TQ = 256
DIFF_MAX_SHIFT = 60.0
DIFF_VT_ROWS = 80
HG_BLOCK = 256
HG_GROUP = 128
HG_SUB = 16
FF_CHUNK = 256
INPROJ_AHEAD = 512
NEG = -1e30


def _cparams(*sem):
    return pltpu.CompilerParams(dimension_semantics=sem, vmem_limit_bytes=VMEM_LIMIT)


def _sigmoid(x):
    return 1.0 / (1.0 + jnp.exp(-x))


def _split3(x):
    hi = x.astype(BF16)
    r = x - hi.astype(F32)
    mid = r.astype(BF16)
    lo = (r - mid.astype(F32)).astype(BF16)
    return hi, mid, lo


def _group_sumsq(x, g):
    sq = x * x
    hi = sq.astype(BF16)
    lo = (sq - hi.astype(F32)).astype(BF16)
    return jnp.dot(hi, g, preferred_element_type=F32) + jnp.dot(lo, g, preferred_element_type=F32)


def _block_diag_ones(width, group):
    i = np.arange(width)
    return jnp.asarray((i[:, None] // group) == (i[None, :] // group), dtype=BF16)


def _prep_kernel(hl_ref, dl_ref, lb_ref, lam_ref, *, depth):
    raw = [hl_ref[l] for l in range(depth)]
    mx = raw[0]
    for l in range(1, depth):
        mx = jnp.maximum(mx, raw[l])
    ex = [jnp.exp(r - mx) for r in raw]
    tot = ex[0]
    for l in range(1, depth):
        tot = tot + ex[l]
    p = [e / tot for e in ex]
    run = p[0]
    lb_ref[0] = run - p[0]
    for l in range(1, depth):
        run = run + p[l]
        lb_ref[l] = run - p[0]
    for l in range(depth):
        d = dl_ref[l]
        a = jnp.sum(d[0:1] * d[1:2], axis=-1, keepdims=True)
        c = jnp.sum(d[2:3] * d[3:4], axis=-1, keepdims=True)
        lam_ref[l] = jnp.broadcast_to(jnp.exp(a) - jnp.exp(c), (1, LANES))


def _prep_params(hg_lower, diff_lambda):
    depth = hg_lower.shape[0]
    return pl.pallas_call(
        functools.partial(_prep_kernel, depth=depth),
        out_shape=(jax.ShapeDtypeStruct(hg_lower.shape, F32),
                   jax.ShapeDtypeStruct((depth, 1, LANES), F32)),
        name="prep_params",
    )(hg_lower.astype(F32), diff_lambda.astype(F32))


def _adaln_kernel(v_ref, w_ref, b_ref, o_ref):
    v = v_ref[...]
    a = v * _sigmoid(v)
    a_hi, a_mid, _ = _split3(a)
    w = w_ref[0]
    w_hi, w_mid, _ = _split3(w)
    acc = jnp.dot(a_hi, w_hi, preferred_element_type=F32)
    acc += jnp.dot(a_mid, w_hi, preferred_element_type=F32)
    acc += jnp.dot(a_hi, w_mid, preferred_element_type=F32)
    o_ref[0] = acc + b_ref[0]


def _adaln(vecs, w_mod, b_mod):
    depth, d, n6 = w_mod.shape
    rows = vecs.shape[0]
    cb = 768
    assert n6 % cb == 0
    return pl.pallas_call(
        _adaln_kernel,
        grid=(depth, n6 // cb),
        in_specs=[pl.BlockSpec((rows, d), lambda l, j: (0, 0)),
                  pl.BlockSpec((1, d, cb), lambda l, j: (l, 0, j)),
                  pl.BlockSpec((1, 1, cb), lambda l, j: (l, 0, j))],
        out_specs=pl.BlockSpec((1, rows, cb), lambda l, j: (l, 0, j)),
        out_shape=jax.ShapeDtypeStruct((depth, rows, n6), F32),
        compiler_params=_cparams("arbitrary", "arbitrary"),
        name="adaln",
    )(vecs, w_mod, b_mod.reshape(depth, 1, n6))


def _rope_tables(L, Lc, dim):
    rows = L // GRID_W
    row = jnp.repeat(jnp.arange(rows, dtype=F32), GRID_W)
    col = jnp.tile(jnp.arange(GRID_W, dtype=F32), rows)
    axis_dim = dim // 2
    n = axis_dim // 2
    inv = jnp.power(ROPE_BASE, -jnp.arange(n, dtype=F32) * 2.0 / axis_dim)
    ar = row[:, None] * inv[None, :]
    ac = col[:, None] * inv[None, :]
    cos = jnp.concatenate([jnp.cos(ar), jnp.cos(ar), jnp.cos(ac), jnp.cos(ac)], axis=-1)
    sin = jnp.concatenate([-jnp.sin(ar), jnp.sin(ar), -jnp.sin(ac), jnp.sin(ac)], axis=-1)
    cos = jnp.concatenate([jnp.ones((Lc, dim), F32), cos], axis=0)
    sin = jnp.concatenate([jnp.zeros((Lc, dim), F32), sin], axis=0)
    rep = LANES // dim
    return jnp.tile(cos, (1, rep)), jnp.tile(sin, (1, rep))


def _rope(x, cos, sin, quarter):
    lane = lax.broadcasted_iota(jnp.int32, x.shape, 1)
    first = (lane % (2 * quarter)) < quarter
    partner = jnp.where(first, pltpu.roll(x, LANES - quarter, 1), pltpu.roll(x, quarter, 1))
    return x * cos + partner * sin


def _inproj_kernel(x_ref, mod_ref, n1g_ref, w_ref, cosa_ref, sina_ref, cosd_ref, sind_ref,
                   gqa_ref, gka_ref, gqd_ref, gkd_ref, lb_ref, g64_ref, g32_ref,
                   qa_ref, ka_ref, va_ref, hqf_ref, hlff_ref, hkf_ref, hqb_ref, hlfb_ref, hkb_ref,
                   hi_ref, hg_ref, qd_ref, kd_ref, vd_ref, p_ref, *, n_batch, n_ctx_tiles):
    b = pl.program_id(0)
    t = pl.program_id(1)
    mrow = jnp.where(t < n_ctx_tiles, n_batch, b)
    m = mod_ref[pl.ds(mrow, 1), :]
    shift, scale = m[:, 0:D_MODEL], m[:, D_MODEL:2 * D_MODEL]

    x = x_ref[0]
    ms = jnp.mean(x * x, axis=-1, keepdims=True)
    h = (x * lax.rsqrt(ms + EPS) * n1g_ref[...]) * (1.0 + scale) + shift
    h = h.astype(BF16)

    edges = sorted(set(_OFF) | {C_AQ + 256})
    issued = [0]

    def proj(c0, width):
        target = min(c0 + width + INPROJ_AHEAD, IN_WIDTH)
        while issued[0] < target:
            e0 = issued[0]
            e1 = min(e for e in edges if e > e0)
            p_ref[:, e0:e1] = jnp.dot(h, w_ref[:, e0:e1], preferred_element_type=F32)
            issued[0] = e1
        return p_ref[:, c0:c0 + width]

    g64 = g64_ref[...]
    g32 = g32_ref[...]
    cosa, sina = cosa_ref[...], sina_ref[...]
    cosd, sind = cosd_ref[...], sind_ref[...]

    for c in range(2):
        p = proj(C_AQ + 256 * c, 256)
        qn = p * lax.rsqrt(_group_sumsq(p, g64) * (1.0 / WIN_HEAD_DIM) + EPS)
        qn = qn * (gqa_ref[:, 256 * c:256 * (c + 1)] * (WIN_HEAD_DIM ** -0.5 * LOG2E))
        for j in range(2):
            r = _rope(qn[:, LANES * j:LANES * (j + 1)], cosa, sina, WIN_HEAD_DIM // 4)
            qa_ref[0, :, 256 * c + LANES * j:256 * c + LANES * (j + 1)] = r.astype(BF16)
    p = proj(C_AK, 128)
    kn = p * lax.rsqrt(_group_sumsq(p, g64[:LANES, :LANES]) * (1.0 / WIN_HEAD_DIM) + EPS) * gka_ref[...]
    ka_ref[0] = _rope(kn, cosa, sina, WIN_HEAD_DIM // 4).astype(BF16)
    va_ref[0] = proj(C_AV, 128).astype(BF16)

    for d, (cq, cf, q_ref, lf_ref, k_ref) in enumerate(((C_BQF, C_BFF, hqf_ref, hlff_ref, hkf_ref),
                                                        (C_BQB, C_BFB, hqb_ref, hlfb_ref, hkb_ref))):
        q_ref[0] = proj(cq, 256)
        lb = lb_ref[d:d + 1, :]
        f = lb + (1.0 - lb) * _sigmoid(proj(cf, 256))
        lf_ref[0] = jnp.log(f)
        k_ref[0] = 1.0 - f
    hi_ref[0] = proj(C_BI, 256).astype(BF16)
    g = proj(C_BG, 256)
    hg_ref[0] = g * _sigmoid(g)

    for c0, gain_ref, o_ref, mult in ((C_CQ, gqd_ref, qd_ref, DIFF_QK_DIM ** -0.5 * LOG2E),
                                      (C_CK, gkd_ref, kd_ref, 1.0)):
        p = proj(c0, 256)
        pn = p * lax.rsqrt(_group_sumsq(p, g32) * (1.0 / DIFF_QK_DIM) + EPS) * (gain_ref[...] * mult)
        for j in range(2):
            r = _rope(pn[:, LANES * j:LANES * (j + 1)], cosd, sind, DIFF_QK_DIM // 4)
            o_ref[0, :, LANES * j:LANES * (j + 1)] = r.astype(BF16)
    vd_ref[0] = proj(C_CV, 256).astype(BF16)


def _inproj(xc, mod_l, n1g, w_in, tabs, gains, lb_l, g64, g32, n_batch, Lc):
    B, Lt, D = xc.shape
    nt = Lt // TOK_TILE
    cosa, sina, cosd, sind = tabs
    gqa, gka, gqd, gkd = gains
    full = lambda a: pl.BlockSpec(a.shape, lambda b, t: (0,) * a.ndim)
    tab = pl.BlockSpec((TOK_TILE, LANES), lambda b, t: (t, 0))
    tok = lambda w: pl.BlockSpec((1, TOK_TILE, w), lambda b, t: (b, t, 0))
    widths = (512, 128, 128, 256, 256, 256, 256, 256, 256, 256, 256, 256, 256, 256)
    dtypes = (BF16, BF16, BF16, F32, F32, F32, F32, F32, F32, BF16, F32, BF16, BF16, BF16)
    return pl.pallas_call(
        functools.partial(_inproj_kernel, n_batch=n_batch, n_ctx_tiles=Lc // TOK_TILE),
        grid=(B, nt),
        in_specs=[tok(D), full(mod_l), full(n1g), full(w_in), tab, tab, tab, tab,
                  full(gqa), full(gka), full(gqd), full(gkd), full(lb_l), full(g64), full(g32)],
        out_specs=[tok(w) for w in widths],
        out_shape=[jax.ShapeDtypeStruct((B, Lt, w), dt) for w, dt in zip(widths, dtypes)],
        scratch_shapes=[pltpu.VMEM((TOK_TILE, IN_WIDTH), F32)],
        compiler_params=_cparams("parallel", "arbitrary"),
        name="inproj",
    )(xc, mod_l, n1g, w_in, cosa, sina, cosd, sind, gqa, gka, gqd, gkd, lb_l, g64, g32)


def _gqa_softmax_pv(s_blocks, v_blocks, sinks):
    rows = s_blocks[0].shape[0]
    per_head = rows // len(sinks)
    rid = lax.broadcasted_iota(jnp.int32, (rows, 1), 0)
    sink = jnp.full((rows, 1), sinks[-1], F32)
    for g in range(len(sinks) - 2, -1, -1):
        sink = jnp.where(rid < (g + 1) * per_head, sinks[g], sink)
    s = jnp.concatenate(s_blocks, axis=1)
    m = jnp.maximum(jnp.max(s, axis=-1, keepdims=True), sink)
    e = jnp.exp2(s - m).astype(BF16)
    pv = jnp.dot(e, jnp.concatenate(v_blocks, axis=0), preferred_element_type=F32)
    den = pv[:, WIN_HEAD_DIM:WIN_HEAD_DIM + 1] + jnp.exp2(sink - m)
    return pv[:, :WIN_HEAD_DIM] / den


def _win_attn_kernel(sink_ref, q_ref, kp_ref, kc_ref, kn_ref, kx_ref, vp_ref, vc_ref, vn_ref, vx_ref, o_ref):
    j = pl.program_id(1)
    nb = pl.num_programs(1)
    group = WIN_HEADS // WIN_KV_HEADS
    rows = group * ATT_BLOCK
    t = lax.broadcasted_iota(jnp.int32, (rows, ATT_BLOCK), 0) % ATT_BLOCK
    s = lax.broadcasted_iota(jnp.int32, (rows, ATT_BLOCK), 1)
    mask_prev = (s >= t) & (j > 0)
    mask_next = (s <= t) & (j < nb - 1)
    def scores(hk):
        q = q_ref[0, hk * group:(hk + 1) * group].reshape(rows, WIN_HEAD_DIM)
        dot = lambda k_ref: jnp.dot(q, k_ref[0, hk], preferred_element_type=F32)
        return [jnp.where(mask_prev, dot(kp_ref), NEG), dot(kc_ref),
                jnp.where(mask_next, dot(kn_ref), NEG), dot(kx_ref)]

    blocks_all = [scores(hk) for hk in range(WIN_KV_HEADS)]
    for hk in range(WIN_KV_HEADS):
        o = _gqa_softmax_pv(blocks_all[hk], [vp_ref[0, hk], vc_ref[0, hk], vn_ref[0, hk], vx_ref[0, hk]],
                            [sink_ref[hk * group + g] for g in range(group)])
        o_ref[0, hk * group:(hk + 1) * group] = o.reshape(group, ATT_BLOCK, WIN_HEAD_DIM).astype(BF16)


def _ctx_attn_kernel(sink_ref, q_ref, kx_ref, vx_ref, o_ref):
    group = WIN_HEADS // WIN_KV_HEADS
    n = q_ref.shape[2]
    for hk in range(WIN_KV_HEADS):
        q = q_ref[0, hk * group:(hk + 1) * group].reshape(group * n, WIN_HEAD_DIM)
        s_ctx = jnp.dot(q, kx_ref[0, hk], preferred_element_type=F32)
        o = _gqa_softmax_pv([s_ctx], [vx_ref[0, hk]], [sink_ref[hk * group + g] for g in range(group)])
        o_ref[0, hk * group:(hk + 1) * group] = o.reshape(group, n, WIN_HEAD_DIM).astype(BF16)


def _win_attention(sink, q_h, kT, v_h, L, Lc, need_ctx):
    B = q_h.shape[0]
    nb = L // ATT_BLOCK
    off = Lc // ATT_BLOCK
    dh = WIN_HEAD_DIM
    smem = pl.BlockSpec(memory_space=pltpu.SMEM)
    kblk = lambda f: pl.BlockSpec((1, WIN_KV_HEADS, dh, ATT_BLOCK), lambda b, j: (b, 0, 0, off + f(j)))
    vblk = lambda f: pl.BlockSpec((1, WIN_KV_HEADS, ATT_BLOCK, LANES), lambda b, j: (b, 0, off + f(j), 0))
    prev = lambda j: jnp.maximum(j - 1, 0)
    cur = lambda j: j
    nxt = lambda j: jnp.minimum(j + 1, nb - 1)
    kctx = pl.BlockSpec((1, WIN_KV_HEADS, dh, Lc), lambda b, j: (b, 0, 0, 0))
    vctx = pl.BlockSpec((1, WIN_KV_HEADS, Lc, LANES), lambda b, j: (b, 0, 0, 0))
    o_lat = pl.pallas_call(
        _win_attn_kernel,
        grid=(B, nb),
        in_specs=[smem, pl.BlockSpec((1, WIN_HEADS, ATT_BLOCK, dh), lambda b, j: (b, 0, off + j, 0)),
                  kblk(prev), kblk(cur), kblk(nxt), kctx, vblk(prev), vblk(cur), vblk(nxt), vctx],
        out_specs=pl.BlockSpec((1, WIN_HEADS, ATT_BLOCK, dh), lambda b, j: (b, 0, j, 0)),
        out_shape=jax.ShapeDtypeStruct((B, WIN_HEADS, L, dh), BF16),
        compiler_params=_cparams("parallel", "arbitrary"),
        name="win_attn",
    )(sink, q_h, kT, kT, kT, kT, v_h, v_h, v_h, v_h)
    if not need_ctx:
        return o_lat, None
    o_ctx = pl.pallas_call(
        _ctx_attn_kernel,
        grid=(B,),
        in_specs=[smem, pl.BlockSpec((1, WIN_HEADS, Lc, dh), lambda b: (b, 0, 0, 0)),
                  pl.BlockSpec((1, WIN_KV_HEADS, dh, Lc), lambda b: (b, 0, 0, 0)),
                  pl.BlockSpec((1, WIN_KV_HEADS, Lc, LANES), lambda b: (b, 0, 0, 0))],
        out_specs=pl.BlockSpec((1, WIN_HEADS, Lc, dh), lambda b: (b, 0, 0, 0)),
        out_shape=jax.ShapeDtypeStruct((B, WIN_HEADS, Lc, dh), BF16),
        compiler_params=_cparams("parallel"),
        name="ctx_attn",
    )(sink, q_h, kT, v_h)
    return o_lat, o_ctx


def _hgrn_kernel(q_ref, lf_ref, k_ref, v_ref, vT_ref, g64_ref, o_ref, st_ref, *, reverse):
    @pl.when(pl.program_id(1) == 0)
    def _():
        st_ref[...] = jnp.zeros_like(st_ref)

    width = HG_HEADS * HG_KEY_DIM
    n_pairs = width // LANES
    n_groups = HG_BLOCK // HG_GROUP
    n_sub = HG_GROUP // HG_SUB
    half = HG_SUB // 2
    ri = lax.broadcasted_iota(jnp.int32, (HG_GROUP, HG_GROUP), 0)
    ci = lax.broadcasted_iota(jnp.int32, (HG_GROUP, HG_GROUP), 1)
    in_sub = (ri // HG_SUB) == (ci // HG_SUB)
    tri = (in_sub & ((ci >= ri) if reverse else (ci <= ri))).astype(BF16)
    rows_full = lax.broadcasted_iota(jnp.int32, (HG_SUB, width), 0)
    rows_half = lax.broadcasted_iota(jnp.int32, (half, width), 0) + (0 if reverse else half)
    same_head = (lax.broadcasted_iota(jnp.int32, (LANES, LANES), 0) // HG_VAL_DIM
                 == lax.broadcasted_iota(jnp.int32, (LANES, LANES), 1) // HG_KEY_DIM)
    g64 = g64_ref[...]
    end_row = 0 if reverse else HG_SUB - 1
    half_lo = 0 if reverse else half

    def is_full(s):
        return (s >= half) if reverse else (s < half)

    def group(gi, carry):
        g = (n_groups - 1 - gi) if reverse else gi
        g0 = pl.multiple_of(g * HG_GROUP, HG_GROUP)
        lf = lf_ref[0, pl.ds(g0, HG_GROUP), :]
        q = q_ref[0, pl.ds(g0, HG_GROUP), :]
        k = k_ref[0, pl.ds(g0, HG_GROUP), :]
        vf = v_ref[0, pl.ds(g0, HG_GROUP), :].astype(F32)
        vT = vT_ref[0, :, pl.ds(g0, HG_GROUP)]

        hi, mid, lo = _split3(lf)
        bcum = (jnp.dot(tri, hi, preferred_element_type=F32) + jnp.dot(tri, mid, preferred_element_type=F32)
                + jnp.dot(tri, lo, preferred_element_type=F32))
        qt = (q * jnp.exp(bcum)).astype(BF16)

        ends = [bcum[c * HG_SUB + end_row:c * HG_SUB + end_row + 1] for c in range(n_sub)]
        bend = jnp.concatenate([jnp.broadcast_to(e, (HG_SUB, width)) for e in ends], axis=0)
        khat = (k * jnp.exp(bend - bcum)).astype(BF16)

        def state_update(c, p):
            r0 = c * HG_SUB
            pieces = [khat[r0:r0 + HG_SUB, p * LANES:(p + 1) * LANES]]
            if r0:
                pieces.insert(0, jnp.zeros((r0, LANES), BF16))
            if r0 + HG_SUB < HG_GROUP:
                pieces.append(jnp.zeros((HG_GROUP - r0 - HG_SUB, LANES), BF16))
            return jnp.dot(vT[p * LANES:(p + 1) * LANES, :], jnp.concatenate(pieces, axis=0),
                           preferred_element_type=F32)

        upds = [[state_update(c, p) for p in range(n_pairs)] for c in range(n_sub)]
        sts = [st_ref[p] for p in range(n_pairs)]
        inter = [None] * n_sub
        for c in (range(n_sub - 1, -1, -1) if reverse else range(n_sub)):
            r0 = c * HG_SUB
            dec = jnp.exp(ends[c])
            o_parts = []
            for p in range(n_pairs):
                ls = slice(p * LANES, (p + 1) * LANES)
                st_b = jnp.where(same_head, sts[p], 0.0).astype(BF16)
                o_parts.append(lax.dot_general(qt[r0:r0 + HG_SUB, ls], st_b, (((1,), (1,)), ((), ())),
                                               preferred_element_type=F32))
                sts[p] = sts[p] * dec[:, ls] + upds[c][p]
            inter[c] = jnp.concatenate(o_parts, axis=1)
        for p in range(n_pairs):
            st_ref[p] = sts[p]

        parts = []
        for c in range(n_sub):
            r0 = c * HG_SUB
            bc, qc, kc = bcum[r0:r0 + HG_SUB], q[r0:r0 + HG_SUB], k[r0:r0 + HG_SUB]
            for s in range(HG_SUB):
                if is_full(s):
                    rows, bt, qq = rows_full, bc, qc
                else:
                    rows, bt, qq = rows_half, bc[half_lo:half_lo + half], qc[half_lo:half_lo + half]
                valid = (rows <= s) if reverse else (rows >= s)
                parts.append(jnp.where(valid, qq * jnp.exp(bt - bc[s:s + 1]) * kc[s:s + 1], 0.0))
        a = jnp.dot(jnp.concatenate(parts, axis=0).astype(BF16), g64, preferred_element_type=F32)
        intra = []
        off = 0
        for c in range(n_sub):
            r0 = c * HG_SUB
            o_full = jnp.zeros((HG_SUB, width), F32)
            o_half = jnp.zeros((half, width), F32)
            for s in range(HG_SUB):
                vs = vf[r0 + s:r0 + s + 1]
                if is_full(s):
                    o_full = o_full + a[off:off + HG_SUB] * vs
                    off += HG_SUB
                else:
                    o_half = o_half + a[off:off + half] * vs
                    off += half
            pad = jnp.zeros((half, width), F32)
            intra.append(o_full + jnp.concatenate([o_half, pad] if reverse else [pad, o_half], axis=0))

        for c in range(n_sub):
            o_ref[0, pl.ds(g0 + c * HG_SUB, HG_SUB), :] = intra[c] + inter[c]
        return carry

    lax.fori_loop(0, n_groups, group, 0)


def _hgrn_scan(q, lf, k, v, vT, g64, Lc, reverse):
    B, Lt, W = q.shape
    nblk = Lt // HG_BLOCK
    nctx = Lc // HG_BLOCK
    if reverse:
        order = lambda i: jnp.where(i < nctx, nctx - 1 - i, nctx + (nblk - 1 - i))
    else:
        order = lambda i: i
    spec = pl.BlockSpec((1, HG_BLOCK, W), lambda b, i: (b, order(i), 0))
    spec_t = pl.BlockSpec((1, W, HG_BLOCK), lambda b, i: (b, 0, order(i)))
    return pl.pallas_call(
        functools.partial(_hgrn_kernel, reverse=reverse),
        grid=(B, nblk),
        in_specs=[spec, spec, spec, spec, spec_t, pl.BlockSpec(g64.shape, lambda b, i: (0, 0))],
        out_specs=spec,
        out_shape=jax.ShapeDtypeStruct((B, Lt, W), F32),
        scratch_shapes=[pltpu.VMEM((W // LANES, LANES, LANES), F32)],
        compiler_params=_cparams("parallel", "arbitrary"),
        name="hgrn_bwd" if reverse else "hgrn_fwd",
    )(q, lf, k, v, vT, g64)


def _diff_attn_kernel(lam_ref, qT_ref, k_ref, vT_ref, kn_ref, og_ref, o_ref, *, key_block, n_keys, post_scale):
    nkb = n_keys // key_block
    n_chain = 2 * DIFF_HEADS
    per_half = LANES // DIFF_QK_DIM
    band = lax.broadcasted_iota(jnp.int32, (LANES, DIFF_TQ), 0) // DIFF_QK_DIM
    ws = []
    for j in range(n_chain):
        half = qT_ref[0, (j // per_half) * LANES:(j // per_half + 1) * LANES, :]
        ws.append(jnp.where(band == j % per_half, half, jnp.zeros_like(half)))

    def scores(kb):
        k0 = pl.multiple_of(kb * key_block, key_block)
        kblks = [k_ref[0, pl.ds(k0, key_block), p * LANES:(p + 1) * LANES] for p in range(n_chain // per_half)]
        sTs = [jnp.dot(kblks[j // per_half], ws[j], preferred_element_type=F32) for j in range(n_chain)]
        vTs = [vT_ref[0, h, :, pl.ds(k0, key_block)] for h in range(DIFF_HEADS)]
        return sTs, vTs

    def pv(vTs, eTs):
        return [jnp.dot(vTs[j // 2], eTs[j], preferred_element_type=F32) for j in range(n_chain)]

    qf = qT_ref[0].astype(F32)
    qn = jnp.sqrt(jnp.sum((qf * qf).reshape(n_chain, DIFF_QK_DIM, DIFF_TQ), axis=1))
    bound = qn * kn_ref[0] * (1.0 + 2.0 ** -10) + 2.0 ** -10
    shift = [bound[j:j + 1] for j in range(n_chain)]
    acc0 = tuple(jnp.zeros((DIFF_VT_ROWS, DIFF_TQ), F32) for _ in range(n_chain))

    def fixed_shift():
        def body(kb, accs):
            sTs, vTs = scores(kb)
            pvs = pv(vTs, [jnp.exp2(sTs[j] - shift[j]).astype(BF16) for j in range(n_chain)])
            return tuple(accs[j] + pvs[j] for j in range(n_chain))
        return lax.fori_loop(0, nkb, body, acc0)

    def running_max():
        def body(kb, carry):
            ms, accs = carry
            sTs, vTs = scores(kb)
            m_new = [jnp.maximum(ms[j], jnp.max(sTs[j], axis=0, keepdims=True)) for j in range(n_chain)]
            pvs = pv(vTs, [jnp.exp2(sTs[j] - m_new[j]).astype(BF16) for j in range(n_chain)])
            accs = tuple(accs[j] * jnp.exp2(ms[j] - m_new[j]) + pvs[j] for j in range(n_chain))
            return tuple(m_new), accs
        m0 = tuple(jnp.full((1, DIFF_TQ), NEG, F32) for _ in range(n_chain))
        return lax.fori_loop(0, nkb, body, (m0, acc0))[1]

    accs = lax.cond(jnp.max(bound) <= DIFF_MAX_SHIFT, fixed_shift, running_max)
    lam = lam_ref[0]
    for h in range(DIFF_HEADS):
        a0, a1 = accs[2 * h], accs[2 * h + 1]
        o = (a0[:DIFF_V_DIM] / a0[DIFF_V_DIM:DIFF_V_DIM + 1]
             - lam * (a1[:DIFF_V_DIM] / a1[DIFF_V_DIM:DIFF_V_DIM + 1]))
        ms = jnp.mean(o * o, axis=0, keepdims=True)
        o_ref[0, h] = (o * lax.rsqrt(ms + EPS) * (og_ref[...] * post_scale)).astype(BF16)


def _pick_key_block(n):
    best = LANES
    for kb in range(LANES, min(n, 1024) + 1, LANES):
        if n % kb == 0:
            best = kb
    return best


def _diff_attention(lam, qT, k, vT_aug, k_norm, og_col, q_off, n_q, n_keys, post_scale):
    B = qT.shape[0]
    width = 2 * DIFF_HEADS * DIFF_QK_DIM
    nq = n_q // DIFF_TQ
    qo = q_off // DIFF_TQ
    n_chain = 2 * DIFF_HEADS
    return pl.pallas_call(
        functools.partial(_diff_attn_kernel, key_block=_pick_key_block(n_keys), n_keys=n_keys,
                          post_scale=post_scale),
        grid=(B, nq),
        in_specs=[pl.BlockSpec(memory_space=pltpu.SMEM),
                  pl.BlockSpec((1, width, DIFF_TQ), lambda b, i: (b, 0, qo + i)),
                  pl.BlockSpec((1, n_keys, width), lambda b, i: (b, 0, 0)),
                  pl.BlockSpec((1, DIFF_HEADS, DIFF_VT_ROWS, n_keys), lambda b, i: (b, 0, 0, 0)),
                  pl.BlockSpec((1, n_chain, 1), lambda b, i: (b, 0, 0)),
                  pl.BlockSpec((DIFF_V_DIM, 1), lambda b, i: (0, 0))],
        out_specs=pl.BlockSpec((1, DIFF_HEADS, DIFF_V_DIM, DIFF_TQ), lambda b, i: (b, 0, 0, i)),
        out_shape=jax.ShapeDtypeStruct((B, DIFF_HEADS, DIFF_V_DIM, n_q), BF16),
        compiler_params=_cparams("parallel", "arbitrary"),
        name="diff_attn",
    )(lam, qT, k, vT_aug, k_norm, og_col)


def _outproj_kernel(x_ref, oa_ref, of_ref, ob_ref, gate_ref, od_ref, mod_ref, w_ref, hog_ref, n2g_ref, g64_ref,
                    xm_ref, h2_ref, *, n_batch, n_ctx_tiles, tile_off):
    b = pl.program_id(0)
    t = pl.program_id(1) + tile_off
    mrow = jnp.where(t < n_ctx_tiles, n_batch, b)
    m = mod_ref[pl.ds(mrow, 1), :]
    gate1 = m[:, 2 * D_MODEL:3 * D_MODEL]
    shift2, scale2 = m[:, 3 * D_MODEL:4 * D_MODEL], m[:, 4 * D_MODEL:5 * D_MODEL]

    o = of_ref[0] + ob_ref[0]
    ss = _group_sumsq(o, g64_ref[...])
    wa, wb = WIN_HEADS * WIN_HEAD_DIM, HG_HEADS * HG_VAL_DIM
    mix = jnp.dot(oa_ref[0], w_ref[0:wa, :], preferred_element_type=F32)
    mix += jnp.dot(od_ref[0], w_ref[wa + wb:, :], preferred_element_type=F32)
    on = o * lax.rsqrt(ss * (1.0 / HG_VAL_DIM) + EPS) * hog_ref[...]
    ob = (on * gate_ref[0]).astype(BF16)
    mix += jnp.dot(ob, w_ref[wa:wa + wb, :], preferred_element_type=F32)

    x = x_ref[0] + gate1 * mix
    xm_ref[0] = x
    ms = jnp.mean(x * x, axis=-1, keepdims=True)
    h = (x * lax.rsqrt(ms + EPS) * n2g_ref[...]) * (1.0 + scale2) + shift2
    h2_ref[0] = h.astype(BF16)


def _outproj(xc, oa, of, ob, gate, od, mod_l, w_out, hog, n2g, g64, n_batch, Lc, tile_off, mix_off):
    B, Lt, D = xc.shape
    nt = Lt // TOK_TILE - tile_off
    full = lambda a: pl.BlockSpec(a.shape, lambda b, t: (0,) * a.ndim)
    comb = lambda w: pl.BlockSpec((1, TOK_TILE, w), lambda b, t: (b, t + tile_off, 0))
    part = lambda w: pl.BlockSpec((1, TOK_TILE, w), lambda b, t: (b, t + tile_off - mix_off, 0))
    outs = pl.BlockSpec((1, TOK_TILE, D), lambda b, t: (b, t, 0))
    return pl.pallas_call(
        functools.partial(_outproj_kernel, n_batch=n_batch, n_ctx_tiles=Lc // TOK_TILE, tile_off=tile_off),
        grid=(B, nt),
        in_specs=[comb(D), part(oa.shape[-1]), comb(256), comb(256), comb(256), part(od.shape[-1]),
                  full(mod_l), full(w_out), full(hog), full(n2g), full(g64)],
        out_specs=[outs, outs],
        out_shape=[jax.ShapeDtypeStruct((B, nt * TOK_TILE, D), F32),
                   jax.ShapeDtypeStruct((B, nt * TOK_TILE, D), BF16)],
        compiler_params=_cparams("parallel", "arbitrary"),
        name="outproj",
    )(xc, oa, of, ob, gate, od, mod_l, w_out, hog, n2g, g64)


def _ffn_kernel(x_ref, h_ref, hp_ref, hn_ref, mod_ref, wu_ref, cw_ref, cb_ref, wd_ref, o_ref, u_ref,
                *, n_batch, n_ctx_tiles, tile_off, n_tiles_total):
    b = pl.program_id(0)
    t = pl.program_id(1) + tile_off
    mrow = jnp.where(t < n_ctx_tiles, n_batch, b)
    gate2 = mod_ref[pl.ds(mrow, 1), :][:, 5 * D_MODEL:6 * D_MODEL]

    prev_ok = jnp.logical_and(t != 0, t != n_ctx_tiles)
    next_ok = jnp.logical_and(t != n_ctx_tiles - 1, t != n_tiles_total - 1)
    hp = jnp.where(prev_ok, hp_ref[0], jnp.zeros_like(hp_ref[0]))
    hn = jnp.where(next_ok, hn_ref[0], jnp.zeros_like(hn_ref[0]))
    hcat = jnp.concatenate([hp, h_ref[0], hn], axis=0)

    lo = BF16_ROWS
    n_chunks = D_FF // FF_CHUNK

    def up_project(c):
        for base in (0, D_FF):
            cols = slice(base + c * FF_CHUNK, base + (c + 1) * FF_CHUNK)
            u_ref[:, cols] = jnp.dot(hcat, wu_ref[:, cols], preferred_element_type=F32)

    def conv(c, base):
        cols = slice(base + c * FF_CHUNK, base + (c + 1) * FF_CHUNK)
        w = cw_ref[:, cols]
        return (cb_ref[:, cols] + u_ref[lo - 1:lo - 1 + TOK_TILE, cols] * w[0:1]
                + u_ref[lo:lo + TOK_TILE, cols] * w[1:2] + u_ref[lo + 1:lo + 1 + TOK_TILE, cols] * w[2:3])

    acc = jnp.zeros((TOK_TILE, D_MODEL), F32)
    ahead = 2
    for c in range(min(ahead, n_chunks)):
        up_project(c)
    for c in range(n_chunks):
        if c + ahead < n_chunks:
            up_project(c + ahead)
        a, val = conv(c, 0), conv(c, D_FF)
        act = (a * _sigmoid(a) * val).astype(BF16)
        acc += jnp.dot(act, wd_ref[c * FF_CHUNK:(c + 1) * FF_CHUNK, :], preferred_element_type=F32)
    o_ref[0] = x_ref[0] + gate2 * acc


def _conv_ffn(xm, h2, mod_l, w_up, conv_w, conv_b, w_down, n_batch, Lc, tile_off, n_tiles_total):
    B, T, D = xm.shape
    nt = T // TOK_TILE
    r = TOK_TILE // BF16_ROWS
    last = T // BF16_ROWS - 1
    full = lambda a: pl.BlockSpec(a.shape, lambda b, t: (0,) * a.ndim)
    tok = pl.BlockSpec((1, TOK_TILE, D), lambda b, t: (b, t, 0))
    halo_prev = pl.BlockSpec((1, BF16_ROWS, D), lambda b, t: (b, jnp.maximum(t * r - 1, 0), 0))
    halo_next = pl.BlockSpec((1, BF16_ROWS, D), lambda b, t: (b, jnp.minimum((t + 1) * r, last), 0))
    return pl.pallas_call(
        functools.partial(_ffn_kernel, n_batch=n_batch, n_ctx_tiles=Lc // TOK_TILE, tile_off=tile_off,
                          n_tiles_total=n_tiles_total),
        grid=(B, nt),
        in_specs=[tok, tok, halo_prev, halo_next, full(mod_l), full(w_up), full(conv_w), full(conv_b),
                  full(w_down)],
        out_specs=tok,
        out_shape=jax.ShapeDtypeStruct((B, T, D), F32),
        scratch_shapes=[pltpu.VMEM((TOK_TILE + 2 * BF16_ROWS, 2 * D_FF), F32)],
        compiler_params=_cparams("parallel", "arbitrary"),
        name="conv_ffn",
    )(xm, h2, h2, h2, mod_l, w_up, conv_w, conv_b, w_down)


def _with_ones_column(v):
    pad = jnp.zeros(v.shape[:-1] + (LANES - v.shape[-1],), v.dtype).at[..., 0].set(1.0)
    return jnp.concatenate([v, pad], axis=-1)


def kernel(x, c, ctx, c_ctx, w_mod, b_mod, norm1_g, norm2_g, w_in, win_qnorm_g, win_knorm_g, win_sink,
           hg_lower, hg_onorm_g, diff_qnorm_g, diff_knorm_g, diff_lambda, diff_onorm_g, w_out,
           w_up, conv_w, conv_b, w_down):
    B, L, D = x.shape
    Lc = ctx.shape[1]
    depth = w_mod.shape[0]
    Lt = Lc + L
    assert D == D_MODEL and L % GRID_W == 0
    assert Lc % TOK_TILE == 0 and L % TOK_TILE == 0 and TOK_TILE % HG_BLOCK == 0 and TOK_TILE % ATT_BLOCK == 0
    assert B + 1 <= SUBLANES

    vecs = jnp.zeros((SUBLANES, D), F32).at[:B].set(c.astype(F32)).at[B].set(c_ctx.astype(F32))
    mod = _adaln(vecs, w_mod.astype(F32), b_mod.astype(F32))
    lower, lam_raw = _prep_params(hg_lower, diff_lambda)

    tabs = _rope_tables(L, Lc, WIN_HEAD_DIM) + _rope_tables(L, Lc, DIFF_QK_DIM)
    g64 = _block_diag_ones(256, 64)
    g32 = _block_diag_ones(256, 32)
    row = lambda v, reps: jnp.tile(v.astype(F32), reps).reshape(1, -1)

    xc = jnp.concatenate([ctx.astype(F32), x.astype(F32)], axis=1)
    n_tiles_total = Lt // TOK_TILE
    nct = Lc // TOK_TILE

    for l in range(depth):
        need_ctx = l < depth - 1
        lam_init = 0.8 - 0.6 * math.exp(-0.3 * l)
        gains = (row(win_qnorm_g[l], WIN_HEADS), row(win_knorm_g[l], WIN_KV_HEADS),
                 row(diff_qnorm_g[l], 2 * DIFF_HEADS), row(diff_knorm_g[l], 2 * DIFF_HEADS))
        (qa, ka, va, hqf, hlff, hkf, hqb, hlfb, hkb, hi, hgate, qd, kd, vd) = _inproj(
            xc, mod[l], norm1_g[l].reshape(1, D).astype(F32), w_in[l].astype(BF16), tabs, gains, lower[l],
            g64, g32, B, Lc)

        qa_h = qa.reshape(B, Lt, WIN_HEADS, WIN_HEAD_DIM).transpose(0, 2, 1, 3)
        kaT = ka.reshape(B, Lt, WIN_KV_HEADS, WIN_HEAD_DIM).transpose(0, 2, 3, 1)
        va_h = _with_ones_column(va.reshape(B, Lt, WIN_KV_HEADS, WIN_HEAD_DIM).transpose(0, 2, 1, 3))
        sink = win_sink[l].astype(F32) * LOG2E
        oa_lat, oa_ctx = _win_attention(sink, qa_h, kaT, va_h, L, Lc, need_ctx)

        hiT = hi.transpose(0, 2, 1)
        of = _hgrn_scan(hqf, hlff, hkf, hi, hiT, g64, Lc, reverse=False)
        ob = _hgrn_scan(hqb, hlfb, hkb, hi, hiT, g64, Lc, reverse=True)

        qdT = qd.transpose(0, 2, 1)
        vdT = vd.reshape(B, Lt, DIFF_HEADS, DIFF_V_DIM).transpose(0, 2, 3, 1)
        fill = jnp.zeros((B, DIFF_HEADS, DIFF_VT_ROWS - DIFF_V_DIM, Lt), BF16).at[:, :, 0].set(1.0)
        vdT_aug = jnp.concatenate([vdT, fill], axis=2)
        lam = lam_raw[l, 0, :1] + lam_init
        dog = diff_onorm_g[l].reshape(DIFF_V_DIM, 1).astype(F32)
        kf = kd.astype(F32).reshape(B, Lt, 2 * DIFF_HEADS, DIFF_QK_DIM)
        k_norm = jnp.sqrt(jnp.max(jnp.sum(kf * kf, axis=-1), axis=1)).reshape(B, 2 * DIFF_HEADS, 1)
        od_lat = _diff_attention(lam, qdT, kd, vdT_aug, k_norm, dog, Lc, L, Lt, 1.0 - lam_init)

        def token_major(o):
            return o.transpose(0, 2, 1, 3).reshape(B, o.shape[2], -1)

        def token_major_t(oT):
            return oT.transpose(0, 3, 1, 2).reshape(B, oT.shape[3], -1)

        if need_ctx:
            od_ctx = _diff_attention(lam, qdT, kd, vdT_aug, k_norm, dog, 0, Lc, Lc, 1.0 - lam_init)
            oa = jnp.concatenate([token_major(oa_ctx), token_major(oa_lat)], axis=1)
            od = jnp.concatenate([token_major_t(od_ctx), token_major_t(od_lat)], axis=1)
            tile_off, mix_off = 0, 0
        else:
            oa, od = token_major(oa_lat), token_major_t(od_lat)
            tile_off, mix_off = nct, nct

        xm, h2 = _outproj(xc, oa, of, ob, hgate, od, mod[l], w_out[l].astype(BF16),
                          row(hg_onorm_g[l], HG_HEADS), norm2_g[l].reshape(1, D).astype(F32), g64,
                          B, Lc, tile_off, mix_off)
        xc = _conv_ffn(xm, h2, mod[l], w_up[l].astype(BF16), conv_w[l].astype(F32),
                       conv_b[l].reshape(1, -1).astype(F32), w_down[l].astype(BF16),
                       B, Lc, tile_off, n_tiles_total)

    return xc
```

```python
import functools
import math

import numpy as np
import jax
import jax.numpy as jnp
from jax import lax
from jax.experimental import pallas as pl
from jax.experimental.pallas import tpu as pltpu

F32 = jnp.float32
BF16 = jnp.bfloat16

D_MODEL = 1024
GRID_W = 64
WIN_HEADS, WIN_KV_HEADS, WIN_HEAD_DIM = 8, 2, 64
WINDOW = 128
HG_HEADS, HG_KEY_DIM, HG_VAL_DIM = 4, 64, 64
DIFF_HEADS, DIFF_QK_DIM, DIFF_V_DIM = 4, 32, 64
D_FF = 11 * D_MODEL // 4
ROPE_BASE = 10000.0
EPS = 1e-6
LOG2E = math.log2(math.e)

_SPLITS = (512, 128, 128, 256, 256, 256, 256, 256, 256, 256, 256, 256)
_OFF = np.concatenate([[0], np.cumsum(_SPLITS)]).tolist()
IN_WIDTH = _OFF[-1]
(C_AQ, C_AK, C_AV, C_BQF, C_BFF, C_BQB, C_BFB, C_BI, C_BG, C_CQ, C_CK, C_CV) = _OFF[:-1]

LANES = 128
SUBLANES = 8
BF16_ROWS = 16
VMEM_LIMIT = 48 * 1024 * 1024

TOK_TILE = 256
ATT_BLOCK = 128
DIFF_TQ = 256
DIFF_MAX_SHIFT = 60.0
DIFF_MAX_KEY_BLOCK = 1408
DIFF_VT_ROWS = 80
HG_BLOCK = 256
HG_GROUP = 128
HG_SUB = 16
FF_CHUNK = 256
FF_AHEAD = 3
INPROJ_AHEAD = 512
NEG = -1e30


def _cparams(*sem):
    return pltpu.CompilerParams(dimension_semantics=sem, vmem_limit_bytes=VMEM_LIMIT)


def _sigmoid(x):
    return 1.0 / (1.0 + jnp.exp(-x))


def _split3(x):
    hi = x.astype(BF16)
    r = x - hi.astype(F32)
    mid = r.astype(BF16)
    lo = (r - mid.astype(F32)).astype(BF16)
    return hi, mid, lo


def _group_sumsq(x, g):
    sq = x * x
    hi = sq.astype(BF16)
    lo = (sq - hi.astype(F32)).astype(BF16)
    return jnp.dot(hi, g, preferred_element_type=F32) + jnp.dot(lo, g, preferred_element_type=F32)


def _block_diag_ones(width, group):
    i = np.arange(width)
    return jnp.asarray((i[:, None] // group) == (i[None, :] // group), dtype=BF16)


def _prep_kernel(hl_ref, dl_ref, lb_ref, lam_ref, *, depth):
    raw = [hl_ref[l] for l in range(depth)]
    mx = raw[0]
    for l in range(1, depth):
        mx = jnp.maximum(mx, raw[l])
    ex = [jnp.exp(r - mx) for r in raw]
    tot = ex[0]
    for l in range(1, depth):
        tot = tot + ex[l]
    p = [e / tot for e in ex]
    run = p[0]
    lb_ref[0] = run - p[0]
    for l in range(1, depth):
        run = run + p[l]
        lb_ref[l] = run - p[0]
    for l in range(depth):
        d = dl_ref[l]
        a = jnp.sum(d[0:1] * d[1:2], axis=-1, keepdims=True)
        c = jnp.sum(d[2:3] * d[3:4], axis=-1, keepdims=True)
        lam_ref[l] = jnp.broadcast_to(jnp.exp(a) - jnp.exp(c), (1, LANES))


def _prep_params(hg_lower, diff_lambda):
    depth = hg_lower.shape[0]
    return pl.pallas_call(
        functools.partial(_prep_kernel, depth=depth),
        out_shape=(jax.ShapeDtypeStruct(hg_lower.shape, F32),
                   jax.ShapeDtypeStruct((depth, 1, LANES), F32)),
        name="prep_params",
    )(hg_lower.astype(F32), diff_lambda.astype(F32))


def _adaln_kernel(v_ref, w_ref, b_ref, o_ref):
    v = v_ref[...]
    a = v * _sigmoid(v)
    a_hi, a_mid, _ = _split3(a)
    w = w_ref[0]
    w_hi, w_mid, _ = _split3(w)
    acc = jnp.dot(a_hi, w_hi, preferred_element_type=F32)
    acc += jnp.dot(a_mid, w_hi, preferred_element_type=F32)
    acc += jnp.dot(a_hi, w_mid, preferred_element_type=F32)
    o_ref[0] = acc + b_ref[0]


def _adaln(vecs, w_mod, b_mod):
    depth, d, n6 = w_mod.shape
    rows = vecs.shape[0]
    cb = 768
    assert n6 % cb == 0
    return pl.pallas_call(
        _adaln_kernel,
        grid=(depth, n6 // cb),
        in_specs=[pl.BlockSpec((rows, d), lambda l, j: (0, 0)),
                  pl.BlockSpec((1, d, cb), lambda l, j: (l, 0, j)),
                  pl.BlockSpec((1, 1, cb), lambda l, j: (l, 0, j))],
        out_specs=pl.BlockSpec((1, rows, cb), lambda l, j: (l, 0, j)),
        out_shape=jax.ShapeDtypeStruct((depth, rows, n6), F32),
        compiler_params=_cparams("arbitrary", "arbitrary"),
        name="adaln",
    )(vecs, w_mod, b_mod.reshape(depth, 1, n6))


def _rope_tables(L, Lc, dim):
    rows = L // GRID_W
    row = jnp.repeat(jnp.arange(rows, dtype=F32), GRID_W)
    col = jnp.tile(jnp.arange(GRID_W, dtype=F32), rows)
    axis_dim = dim // 2
    n = axis_dim // 2
    inv = jnp.power(ROPE_BASE, -jnp.arange(n, dtype=F32) * 2.0 / axis_dim)
    ar = row[:, None] * inv[None, :]
    ac = col[:, None] * inv[None, :]
    cos = jnp.concatenate([jnp.cos(ar), jnp.cos(ar), jnp.cos(ac), jnp.cos(ac)], axis=-1)
    sin = jnp.concatenate([-jnp.sin(ar), jnp.sin(ar), -jnp.sin(ac), jnp.sin(ac)], axis=-1)
    cos = jnp.concatenate([jnp.ones((Lc, dim), F32), cos], axis=0)
    sin = jnp.concatenate([jnp.zeros((Lc, dim), F32), sin], axis=0)
    rep = LANES // dim
    return jnp.tile(cos, (1, rep)), jnp.tile(sin, (1, rep))


def _rope(x, cos, sin, quarter):
    lane = lax.broadcasted_iota(jnp.int32, x.shape, 1)
    first = (lane % (2 * quarter)) < quarter
    partner = jnp.where(first, pltpu.roll(x, LANES - quarter, 1), pltpu.roll(x, quarter, 1))
    return x * cos + partner * sin


def _inproj_kernel(x_ref, mod_ref, n1g_ref, w_ref, cosa_ref, sina_ref, cosd_ref, sind_ref,
                   gqa_ref, gka_ref, gqd_ref, gkd_ref, lb_ref, g64_ref, g32_ref,
                   qa_ref, ka_ref, va_ref, hqf_ref, hlff_ref, hkf_ref, hqb_ref, hlfb_ref, hkb_ref,
                   hi_ref, hg_ref, qd_ref, kd_ref, vd_ref, p_ref, *, n_batch, n_ctx_tiles):
    b = pl.program_id(0)
    t = pl.program_id(1)
    mrow = jnp.where(t < n_ctx_tiles, n_batch, b)
    m = mod_ref[pl.ds(mrow, 1), :]
    shift, scale = m[:, 0:D_MODEL], m[:, D_MODEL:2 * D_MODEL]

    x = x_ref[0]
    ms = jnp.mean(x * x, axis=-1, keepdims=True)
    h = (x * lax.rsqrt(ms + EPS) * n1g_ref[...]) * (1.0 + scale) + shift
    h = h.astype(BF16)

    edges = sorted(set(_OFF) | {C_AQ + 256})
    issued = [0]

    def proj(c0, width):
        target = min(c0 + width + INPROJ_AHEAD, IN_WIDTH)
        while issued[0] < target:
            e0 = issued[0]
            e1 = min(e for e in edges if e > e0)
            p_ref[:, e0:e1] = jnp.dot(h, w_ref[:, e0:e1], preferred_element_type=F32)
            issued[0] = e1
        return p_ref[:, c0:c0 + width]

    g64 = g64_ref[...]
    g32 = g32_ref[...]
    cosa, sina = cosa_ref[...], sina_ref[...]
    cosd, sind = cosd_ref[...], sind_ref[...]

    for c in range(2):
        p = proj(C_AQ + 256 * c, 256)
        qn = p * lax.rsqrt(_group_sumsq(p, g64) * (1.0 / WIN_HEAD_DIM) + EPS)
        qn = qn * (gqa_ref[:, 256 * c:256 * (c + 1)] * (WIN_HEAD_DIM ** -0.5 * LOG2E))
        for j in range(2):
            r = _rope(qn[:, LANES * j:LANES * (j + 1)], cosa, sina, WIN_HEAD_DIM // 4)
            qa_ref[0, :, 256 * c + LANES * j:256 * c + LANES * (j + 1)] = r.astype(BF16)
    p = proj(C_AK, 128)
    kn = p * lax.rsqrt(_group_sumsq(p, g64[:LANES, :LANES]) * (1.0 / WIN_HEAD_DIM) + EPS) * gka_ref[...]
    ka_ref[0] = _rope(kn, cosa, sina, WIN_HEAD_DIM // 4).astype(BF16)
    va_ref[0] = proj(C_AV, 128).astype(BF16)

    for d, (cq, cf, q_ref, lf_ref, k_ref) in enumerate(((C_BQF, C_BFF, hqf_ref, hlff_ref, hkf_ref),
                                                        (C_BQB, C_BFB, hqb_ref, hlfb_ref, hkb_ref))):
        q_ref[0] = proj(cq, 256)
        lb = lb_ref[d:d + 1, :]
        f = lb + (1.0 - lb) * _sigmoid(proj(cf, 256))
        lf_ref[0] = jnp.log(f)
        k_ref[0] = 1.0 - f
    hi_ref[0] = proj(C_BI, 256).astype(BF16)
    g = proj(C_BG, 256)
    hg_ref[0] = g * _sigmoid(g)

    for c0, gain_ref, o_ref, mult in ((C_CQ, gqd_ref, qd_ref, DIFF_QK_DIM ** -0.5 * LOG2E),
                                      (C_CK, gkd_ref, kd_ref, 1.0)):
        p = proj(c0, 256)
        pn = p * lax.rsqrt(_group_sumsq(p, g32) * (1.0 / DIFF_QK_DIM) + EPS) * (gain_ref[...] * mult)
        for j in range(2):
            r = _rope(pn[:, LANES * j:LANES * (j + 1)], cosd, sind, DIFF_QK_DIM // 4)
            o_ref[0, :, LANES * j:LANES * (j + 1)] = r.astype(BF16)
    vd_ref[0] = proj(C_CV, 256).astype(BF16)


def _inproj(xc, mod_l, n1g, w_in, tabs, gains, lb_l, g64, g32, n_batch, Lc):
    B, Lt, D = xc.shape
    nt = Lt // TOK_TILE
    cosa, sina, cosd, sind = tabs
    gqa, gka, gqd, gkd = gains
    full = lambda a: pl.BlockSpec(a.shape, lambda b, t: (0,) * a.ndim)
    tab = pl.BlockSpec((TOK_TILE, LANES), lambda b, t: (t, 0))
    tok = lambda w: pl.BlockSpec((1, TOK_TILE, w), lambda b, t: (b, t, 0))
    widths = (512, 128, 128, 256, 256, 256, 256, 256, 256, 256, 256, 256, 256, 256)
    dtypes = (BF16, BF16, BF16, F32, F32, F32, F32, F32, F32, BF16, F32, BF16, BF16, BF16)
    return pl.pallas_call(
        functools.partial(_inproj_kernel, n_batch=n_batch, n_ctx_tiles=Lc // TOK_TILE),
        grid=(B, nt),
        in_specs=[tok(D), full(mod_l), full(n1g), full(w_in), tab, tab, tab, tab,
                  full(gqa), full(gka), full(gqd), full(gkd), full(lb_l), full(g64), full(g32)],
        out_specs=[tok(w) for w in widths],
        out_shape=[jax.ShapeDtypeStruct((B, Lt, w), dt) for w, dt in zip(widths, dtypes)],
        scratch_shapes=[pltpu.VMEM((TOK_TILE, IN_WIDTH), F32)],
        compiler_params=_cparams("parallel", "arbitrary"),
        name="inproj",
    )(xc, mod_l, n1g, w_in, cosa, sina, cosd, sind, gqa, gka, gqd, gkd, lb_l, g64, g32)


def _gqa_softmax_pv(s_blocks, v_blocks, sinks):
    rows = s_blocks[0].shape[0]
    per_head = rows // len(sinks)
    rid = lax.broadcasted_iota(jnp.int32, (rows, 1), 0)
    sink = jnp.full((rows, 1), sinks[-1], F32)
    for g in range(len(sinks) - 2, -1, -1):
        sink = jnp.where(rid < (g + 1) * per_head, sinks[g], sink)
    s = jnp.concatenate(s_blocks, axis=1)
    m = jnp.maximum(jnp.max(s, axis=-1, keepdims=True), sink)
    e = jnp.exp2(s - m).astype(BF16)
    pv = jnp.dot(e, jnp.concatenate(v_blocks, axis=0), preferred_element_type=F32)
    den = pv[:, WIN_HEAD_DIM:WIN_HEAD_DIM + 1] + jnp.exp2(sink - m)
    return pv[:, :WIN_HEAD_DIM] / den


def _win_attn_kernel(sink_ref, q_ref, kp_ref, kc_ref, kn_ref, kx_ref, vp_ref, vc_ref, vn_ref, vx_ref, o_ref):
    j = pl.program_id(1)
    nb = pl.num_programs(1)
    group = WIN_HEADS // WIN_KV_HEADS
    rows = group * ATT_BLOCK
    t = lax.broadcasted_iota(jnp.int32, (rows, ATT_BLOCK), 0) % ATT_BLOCK
    s = lax.broadcasted_iota(jnp.int32, (rows, ATT_BLOCK), 1)
    mask_prev = (s >= t) & (j > 0)
    mask_next = (s <= t) & (j < nb - 1)
    def scores(hk):
        q = q_ref[0, hk * group:(hk + 1) * group].reshape(rows, WIN_HEAD_DIM)
        dot = lambda k_ref: jnp.dot(q, k_ref[0, hk], preferred_element_type=F32)
        return [jnp.where(mask_prev, dot(kp_ref), NEG), dot(kc_ref),
                jnp.where(mask_next, dot(kn_ref), NEG), dot(kx_ref)]

    blocks_all = [scores(hk) for hk in range(WIN_KV_HEADS)]
    for hk in range(WIN_KV_HEADS):
        o = _gqa_softmax_pv(blocks_all[hk], [vp_ref[0, hk], vc_ref[0, hk], vn_ref[0, hk], vx_ref[0, hk]],
                            [sink_ref[hk * group + g] for g in range(group)])
        o_ref[0, hk * group:(hk + 1) * group] = o.reshape(group, ATT_BLOCK, WIN_HEAD_DIM).astype(BF16)


def _ctx_attn_kernel(sink_ref, q_ref, kx_ref, vx_ref, o_ref):
    group = WIN_HEADS // WIN_KV_HEADS
    n = q_ref.shape[2]
    for hk in range(WIN_KV_HEADS):
        q = q_ref[0, hk * group:(hk + 1) * group].reshape(group * n, WIN_HEAD_DIM)
        s_ctx = jnp.dot(q, kx_ref[0, hk], preferred_element_type=F32)
        o = _gqa_softmax_pv([s_ctx], [vx_ref[0, hk]], [sink_ref[hk * group + g] for g in range(group)])
        o_ref[0, hk * group:(hk + 1) * group] = o.reshape(group, n, WIN_HEAD_DIM).astype(BF16)


def _win_attention(sink, q_h, kT, v_h, L, Lc, need_ctx):
    B = q_h.shape[0]
    nb = L // ATT_BLOCK
    off = Lc // ATT_BLOCK
    dh = WIN_HEAD_DIM
    smem = pl.BlockSpec(memory_space=pltpu.SMEM)
    kblk = lambda f: pl.BlockSpec((1, WIN_KV_HEADS, dh, ATT_BLOCK), lambda b, j: (b, 0, 0, off + f(j)))
    vblk = lambda f: pl.BlockSpec((1, WIN_KV_HEADS, ATT_BLOCK, LANES), lambda b, j: (b, 0, off + f(j), 0))
    prev = lambda j: jnp.maximum(j - 1, 0)
    cur = lambda j: j
    nxt = lambda j: jnp.minimum(j + 1, nb - 1)
    kctx = pl.BlockSpec((1, WIN_KV_HEADS, dh, Lc), lambda b, j: (b, 0, 0, 0))
    vctx = pl.BlockSpec((1, WIN_KV_HEADS, Lc, LANES), lambda b, j: (b, 0, 0, 0))
    o_lat = pl.pallas_call(
        _win_attn_kernel,
        grid=(B, nb),
        in_specs=[smem, pl.BlockSpec((1, WIN_HEADS, ATT_BLOCK, dh), lambda b, j: (b, 0, off + j, 0)),
                  kblk(prev), kblk(cur), kblk(nxt), kctx, vblk(prev), vblk(cur), vblk(nxt), vctx],
        out_specs=pl.BlockSpec((1, WIN_HEADS, ATT_BLOCK, dh), lambda b, j: (b, 0, j, 0)),
        out_shape=jax.ShapeDtypeStruct((B, WIN_HEADS, L, dh), BF16),
        compiler_params=_cparams("parallel", "arbitrary"),
        name="win_attn",
    )(sink, q_h, kT, kT, kT, kT, v_h, v_h, v_h, v_h)
    if not need_ctx:
        return o_lat, None
    o_ctx = pl.pallas_call(
        _ctx_attn_kernel,
        grid=(B,),
        in_specs=[smem, pl.BlockSpec((1, WIN_HEADS, Lc, dh), lambda b: (b, 0, 0, 0)),
                  pl.BlockSpec((1, WIN_KV_HEADS, dh, Lc), lambda b: (b, 0, 0, 0)),
                  pl.BlockSpec((1, WIN_KV_HEADS, Lc, LANES), lambda b: (b, 0, 0, 0))],
        out_specs=pl.BlockSpec((1, WIN_HEADS, Lc, dh), lambda b: (b, 0, 0, 0)),
        out_shape=jax.ShapeDtypeStruct((B, WIN_HEADS, Lc, dh), BF16),
        compiler_params=_cparams("parallel"),
        name="ctx_attn",
    )(sink, q_h, kT, v_h)
    return o_lat, o_ctx


def _hgrn_direction(q_ref, lf_ref, k_ref, v_ref, vT_ref, g64_ref, o_ref, st_ref, *, reverse):
    @pl.when(pl.program_id(1) == 0)
    def _():
        st_ref[...] = jnp.zeros_like(st_ref)

    width = HG_HEADS * HG_KEY_DIM
    n_pairs = width // LANES
    n_groups = HG_BLOCK // HG_GROUP
    n_sub = HG_GROUP // HG_SUB
    half = HG_SUB // 2
    ri = lax.broadcasted_iota(jnp.int32, (HG_GROUP, HG_GROUP), 0)
    ci = lax.broadcasted_iota(jnp.int32, (HG_GROUP, HG_GROUP), 1)
    in_sub = (ri // HG_SUB) == (ci // HG_SUB)
    tri = (in_sub & ((ci >= ri) if reverse else (ci <= ri))).astype(BF16)
    rows_full = lax.broadcasted_iota(jnp.int32, (HG_SUB, width), 0)
    rows_half = lax.broadcasted_iota(jnp.int32, (half, width), 0) + (0 if reverse else half)
    same_head = (lax.broadcasted_iota(jnp.int32, (LANES, LANES), 0) // HG_VAL_DIM
                 == lax.broadcasted_iota(jnp.int32, (LANES, LANES), 1) // HG_KEY_DIM)
    g64 = g64_ref[...]
    end_row = 0 if reverse else HG_SUB - 1
    half_lo = 0 if reverse else half

    def is_full(s):
        return (s >= half) if reverse else (s < half)

    def group(gi):
        g = (n_groups - 1 - gi) if reverse else gi
        g0 = pl.multiple_of(g * HG_GROUP, HG_GROUP)
        lf = lf_ref[0, pl.ds(g0, HG_GROUP), :]
        q = q_ref[0, pl.ds(g0, HG_GROUP), :]
        k = k_ref[0, pl.ds(g0, HG_GROUP), :]
        vf = v_ref[0, pl.ds(g0, HG_GROUP), :].astype(F32)
        vT = vT_ref[0, :, pl.ds(g0, HG_GROUP)]

        hi, mid, lo = _split3(lf)
        bcum = (jnp.dot(tri, hi, preferred_element_type=F32) + jnp.dot(tri, mid, preferred_element_type=F32)
                + jnp.dot(tri, lo, preferred_element_type=F32))
        qt = (q * jnp.exp(bcum)).astype(BF16)

        ends = [bcum[c * HG_SUB + end_row:c * HG_SUB + end_row + 1] for c in range(n_sub)]
        bend = jnp.concatenate([jnp.broadcast_to(e, (HG_SUB, width)) for e in ends], axis=0)
        khat = (k * jnp.exp(bend - bcum)).astype(BF16)

        def state_update(c, p):
            r0 = c * HG_SUB
            pieces = [khat[r0:r0 + HG_SUB, p * LANES:(p + 1) * LANES]]
            if r0:
                pieces.insert(0, jnp.zeros((r0, LANES), BF16))
            if r0 + HG_SUB < HG_GROUP:
                pieces.append(jnp.zeros((HG_GROUP - r0 - HG_SUB, LANES), BF16))
            return jnp.dot(vT[p * LANES:(p + 1) * LANES, :], jnp.concatenate(pieces, axis=0),
                           preferred_element_type=F32)

        upds = [[state_update(c, p) for p in range(n_pairs)] for c in range(n_sub)]
        sts = [st_ref[p] for p in range(n_pairs)]
        inter = [None] * n_sub
        for c in (range(n_sub - 1, -1, -1) if reverse else range(n_sub)):
            r0 = c * HG_SUB
            dec = jnp.exp(ends[c])
            o_parts = []
            for p in range(n_pairs):
                ls = slice(p * LANES, (p + 1) * LANES)
                st_b = jnp.where(same_head, sts[p], 0.0).astype(BF16)
                o_parts.append(lax.dot_general(qt[r0:r0 + HG_SUB, ls], st_b, (((1,), (1,)), ((), ())),
                                               preferred_element_type=F32))
                sts[p] = sts[p] * dec[:, ls] + upds[c][p]
            inter[c] = jnp.concatenate(o_parts, axis=1)
        for p in range(n_pairs):
            st_ref[p] = sts[p]

        parts = []
        for c in range(n_sub):
            r0 = c * HG_SUB
            bc, qc, kc = bcum[r0:r0 + HG_SUB], q[r0:r0 + HG_SUB], k[r0:r0 + HG_SUB]
            for s in range(HG_SUB):
                if is_full(s):
                    rows, bt, qq = rows_full, bc, qc
                else:
                    rows, bt, qq = rows_half, bc[half_lo:half_lo + half], qc[half_lo:half_lo + half]
                valid = (rows <= s) if reverse else (rows >= s)
                parts.append(jnp.where(valid, qq * jnp.exp(bt - bc[s:s + 1]) * kc[s:s + 1], 0.0))
        a = jnp.dot(jnp.concatenate(parts, axis=0).astype(BF16), g64, preferred_element_type=F32)
        intra = []
        off = 0
        for c in range(n_sub):
            r0 = c * HG_SUB
            o_full = jnp.zeros((HG_SUB, width), F32)
            o_half = jnp.zeros((half, width), F32)
            for s in range(HG_SUB):
                vs = vf[r0 + s:r0 + s + 1]
                if is_full(s):
                    o_full = o_full + a[off:off + HG_SUB] * vs
                    off += HG_SUB
                else:
                    o_half = o_half + a[off:off + half] * vs
                    off += half
            pad = jnp.zeros((half, width), F32)
            intra.append(o_full + jnp.concatenate([o_half, pad] if reverse else [pad, o_half], axis=0))

        for c in range(n_sub):
            o_ref[0, pl.ds(g0 + c * HG_SUB, HG_SUB), :] = intra[c] + inter[c]

    return group


def _hgrn_kernel(qf_ref, lff_ref, kf_ref, vf_ref, vTf_ref, qb_ref, lfb_ref, kb_ref, vb_ref, vTb_ref, g64_ref,
                 of_ref, ob_ref, stf_ref, stb_ref):
    fwd = _hgrn_direction(qf_ref, lff_ref, kf_ref, vf_ref, vTf_ref, g64_ref, of_ref, stf_ref, reverse=False)
    bwd = _hgrn_direction(qb_ref, lfb_ref, kb_ref, vb_ref, vTb_ref, g64_ref, ob_ref, stb_ref, reverse=True)

    def body(gi, carry):
        fwd(gi)
        bwd(gi)
        return carry

    lax.fori_loop(0, HG_BLOCK // HG_GROUP, body, 0)


def _hgrn_scan(qf, lff, kf, qb, lfb, kb, v, vT, g64, Lc):
    B, Lt, W = qf.shape
    nblk = Lt // HG_BLOCK
    nctx = Lc // HG_BLOCK
    rev = lambda i: jnp.where(i < nctx, nctx - 1 - i, nctx + (nblk - 1 - i))
    fspec = pl.BlockSpec((1, HG_BLOCK, W), lambda b, i: (b, i, 0))
    fspec_t = pl.BlockSpec((1, W, HG_BLOCK), lambda b, i: (b, 0, i))
    bspec = pl.BlockSpec((1, HG_BLOCK, W), lambda b, i: (b, rev(i), 0))
    bspec_t = pl.BlockSpec((1, W, HG_BLOCK), lambda b, i: (b, 0, rev(i)))
    state = pltpu.VMEM((W // LANES, LANES, LANES), F32)
    return pl.pallas_call(
        _hgrn_kernel,
        grid=(B, nblk),
        in_specs=[fspec, fspec, fspec, fspec, fspec_t, bspec, bspec, bspec, bspec, bspec_t,
                  pl.BlockSpec(g64.shape, lambda b, i: (0, 0))],
        out_specs=[fspec, bspec],
        out_shape=[jax.ShapeDtypeStruct((B, Lt, W), F32)] * 2,
        scratch_shapes=[state, state],
        compiler_params=_cparams("parallel", "arbitrary"),
        name="hgrn",
    )(qf, lff, kf, v, vT, qb, lfb, kb, v, vT, g64)


def _diff_attn_kernel(lam_ref, qT_ref, k_ref, vT_ref, kn_ref, og_ref, o_ref, *, key_block, n_keys, post_scale):
    nkb = n_keys // key_block
    n_chain = 2 * DIFF_HEADS
    per_half = LANES // DIFF_QK_DIM
    band = lax.broadcasted_iota(jnp.int32, (LANES, DIFF_TQ), 0) // DIFF_QK_DIM
    ws = []
    for j in range(n_chain):
        half = qT_ref[0, (j // per_half) * LANES:(j // per_half + 1) * LANES, :]
        ws.append(jnp.where(band == j % per_half, half, jnp.zeros_like(half)))

    def scores(kb):
        k0 = pl.multiple_of(kb * key_block, key_block)
        kblks = [k_ref[0, pl.ds(k0, key_block), p * LANES:(p + 1) * LANES] for p in range(n_chain // per_half)]
        sTs = [jnp.dot(kblks[j // per_half], ws[j], preferred_element_type=F32) for j in range(n_chain)]
        vTs = [vT_ref[0, h, :, pl.ds(k0, key_block)] for h in range(DIFF_HEADS)]
        return sTs, vTs

    def pv(vTs, eTs):
        return [jnp.dot(vTs[j // 2], eTs[j], preferred_element_type=F32) for j in range(n_chain)]

    qf = qT_ref[0].astype(F32)
    qn = jnp.sqrt(jnp.sum((qf * qf).reshape(n_chain, DIFF_QK_DIM, DIFF_TQ), axis=1))
    bound = qn * kn_ref[0] * (1.0 + 2.0 ** -10) + 2.0 ** -10
    shift = [bound[j:j + 1] for j in range(n_chain)]
    acc0 = tuple(jnp.zeros((DIFF_VT_ROWS, DIFF_TQ), F32) for _ in range(n_chain))

    def fixed_shift():
        def body(kb, accs):
            sTs, vTs = scores(kb)
            pvs = pv(vTs, [jnp.exp2(sTs[j] - shift[j]).astype(BF16) for j in range(n_chain)])
            return tuple(accs[j] + pvs[j] for j in range(n_chain))
        return lax.fori_loop(0, nkb, body, acc0)

    def running_max():
        def body(kb, carry):
            ms, accs = carry
            sTs, vTs = scores(kb)
            m_new = [jnp.maximum(ms[j], jnp.max(sTs[j], axis=0, keepdims=True)) for j in range(n_chain)]
            pvs = pv(vTs, [jnp.exp2(sTs[j] - m_new[j]).astype(BF16) for j in range(n_chain)])
            accs = tuple(accs[j] * jnp.exp2(ms[j] - m_new[j]) + pvs[j] for j in range(n_chain))
            return tuple(m_new), accs
        m0 = tuple(jnp.full((1, DIFF_TQ), NEG, F32) for _ in range(n_chain))
        return lax.fori_loop(0, nkb, body, (m0, acc0))[1]

    accs = lax.cond(jnp.max(bound) <= DIFF_MAX_SHIFT, fixed_shift, running_max)
    lam = lam_ref[0]
    for h in range(DIFF_HEADS):
        a0, a1 = accs[2 * h], accs[2 * h + 1]
        o = (a0[:DIFF_V_DIM] / a0[DIFF_V_DIM:DIFF_V_DIM + 1]
             - lam * (a1[:DIFF_V_DIM] / a1[DIFF_V_DIM:DIFF_V_DIM + 1]))
        ms = jnp.mean(o * o, axis=0, keepdims=True)
        o_ref[0, h] = (o * lax.rsqrt(ms + EPS) * (og_ref[...] * post_scale)).astype(BF16)


def _pick_key_block(n):
    best = LANES
    for kb in range(LANES, min(n, DIFF_MAX_KEY_BLOCK) + 1, LANES):
        if n % kb == 0:
            best = kb
    return best


def _diff_attention(lam, qT, k, vT_aug, k_norm, og_col, q_off, n_q, n_keys, post_scale):
    B = qT.shape[0]
    width = 2 * DIFF_HEADS * DIFF_QK_DIM
    nq = n_q // DIFF_TQ
    qo = q_off // DIFF_TQ
    n_chain = 2 * DIFF_HEADS
    return pl.pallas_call(
        functools.partial(_diff_attn_kernel, key_block=_pick_key_block(n_keys), n_keys=n_keys,
                          post_scale=post_scale),
        grid=(B, nq),
        in_specs=[pl.BlockSpec(memory_space=pltpu.SMEM),
                  pl.BlockSpec((1, width, DIFF_TQ), lambda b, i: (b, 0, qo + i)),
                  pl.BlockSpec((1, n_keys, width), lambda b, i: (b, 0, 0)),
                  pl.BlockSpec((1, DIFF_HEADS, DIFF_VT_ROWS, n_keys), lambda b, i: (b, 0, 0, 0)),
                  pl.BlockSpec((1, n_chain, 1), lambda b, i: (b, 0, 0)),
                  pl.BlockSpec((DIFF_V_DIM, 1), lambda b, i: (0, 0))],
        out_specs=pl.BlockSpec((1, DIFF_HEADS, DIFF_V_DIM, DIFF_TQ), lambda b, i: (b, 0, 0, i)),
        out_shape=jax.ShapeDtypeStruct((B, DIFF_HEADS, DIFF_V_DIM, n_q), BF16),
        compiler_params=_cparams("parallel", "arbitrary"),
        name="diff_attn",
    )(lam, qT, k, vT_aug, k_norm, og_col)


def _outproj_kernel(x_ref, oa_ref, of_ref, ob_ref, gate_ref, od_ref, mod_ref, w_ref, hog_ref, n2g_ref, g64_ref,
                    xm_ref, h2_ref, *, n_batch, n_ctx_tiles, tile_off):
    b = pl.program_id(0)
    t = pl.program_id(1) + tile_off
    mrow = jnp.where(t < n_ctx_tiles, n_batch, b)
    m = mod_ref[pl.ds(mrow, 1), :]
    gate1 = m[:, 2 * D_MODEL:3 * D_MODEL]
    shift2, scale2 = m[:, 3 * D_MODEL:4 * D_MODEL], m[:, 4 * D_MODEL:5 * D_MODEL]

    o = of_ref[0] + ob_ref[0]
    ss = _group_sumsq(o, g64_ref[...])
    wa, wb = WIN_HEADS * WIN_HEAD_DIM, HG_HEADS * HG_VAL_DIM
    mix = jnp.dot(oa_ref[0], w_ref[0:wa, :], preferred_element_type=F32)
    mix += jnp.dot(od_ref[0], w_ref[wa + wb:, :], preferred_element_type=F32)
    on = o * lax.rsqrt(ss * (1.0 / HG_VAL_DIM) + EPS) * hog_ref[...]
    ob = (on * gate_ref[0]).astype(BF16)
    mix += jnp.dot(ob, w_ref[wa:wa + wb, :], preferred_element_type=F32)

    x = x_ref[0] + gate1 * mix
    xm_ref[0] = x
    ms = jnp.mean(x * x, axis=-1, keepdims=True)
    h = (x * lax.rsqrt(ms + EPS) * n2g_ref[...]) * (1.0 + scale2) + shift2
    h2_ref[0] = h.astype(BF16)


def _outproj(xc, oa, of, ob, gate, od, mod_l, w_out, hog, n2g, g64, n_batch, Lc, tile_off, mix_off):
    B, Lt, D = xc.shape
    nt = Lt // TOK_TILE - tile_off
    full = lambda a: pl.BlockSpec(a.shape, lambda b, t: (0,) * a.ndim)
    comb = lambda w: pl.BlockSpec((1, TOK_TILE, w), lambda b, t: (b, t + tile_off, 0))
    part = lambda w: pl.BlockSpec((1, TOK_TILE, w), lambda b, t: (b, t + tile_off - mix_off, 0))
    outs = pl.BlockSpec((1, TOK_TILE, D), lambda b, t: (b, t, 0))
    return pl.pallas_call(
        functools.partial(_outproj_kernel, n_batch=n_batch, n_ctx_tiles=Lc // TOK_TILE, tile_off=tile_off),
        grid=(B, nt),
        in_specs=[comb(D), part(oa.shape[-1]), comb(256), comb(256), comb(256), part(od.shape[-1]),
                  full(mod_l), full(w_out), full(hog), full(n2g), full(g64)],
        out_specs=[outs, outs],
        out_shape=[jax.ShapeDtypeStruct((B, nt * TOK_TILE, D), F32),
                   jax.ShapeDtypeStruct((B, nt * TOK_TILE, D), BF16)],
        compiler_params=_cparams("parallel", "arbitrary"),
        name="outproj",
    )(xc, oa, of, ob, gate, od, mod_l, w_out, hog, n2g, g64)


def _ffn_kernel(x_ref, h_ref, hp_ref, hn_ref, mod_ref, wu_ref, cw_ref, cb_ref, wd_ref, o_ref, u_ref,
                *, n_batch, n_ctx_tiles, tile_off, n_tiles_total):
    b = pl.program_id(0)
    t = pl.program_id(1) + tile_off
    mrow = jnp.where(t < n_ctx_tiles, n_batch, b)
    gate2 = mod_ref[pl.ds(mrow, 1), :][:, 5 * D_MODEL:6 * D_MODEL]

    prev_ok = jnp.logical_and(t != 0, t != n_ctx_tiles)
    next_ok = jnp.logical_and(t != n_ctx_tiles - 1, t != n_tiles_total - 1)
    hp = jnp.where(prev_ok, hp_ref[0], jnp.zeros_like(hp_ref[0]))
    hn = jnp.where(next_ok, hn_ref[0], jnp.zeros_like(hn_ref[0]))
    hcat = jnp.concatenate([hp, h_ref[0], hn], axis=0)

    lo = BF16_ROWS
    n_chunks = D_FF // FF_CHUNK

    def up_project(c):
        for base in (0, D_FF):
            cols = slice(base + c * FF_CHUNK, base + (c + 1) * FF_CHUNK)
            u_ref[:, cols] = jnp.dot(hcat, wu_ref[:, cols], preferred_element_type=F32)

    def conv(c, base):
        cols = slice(base + c * FF_CHUNK, base + (c + 1) * FF_CHUNK)
        w = cw_ref[:, cols]
        return (cb_ref[:, cols] + u_ref[lo - 1:lo - 1 + TOK_TILE, cols] * w[0:1]
                + u_ref[lo:lo + TOK_TILE, cols] * w[1:2] + u_ref[lo + 1:lo + 1 + TOK_TILE, cols] * w[2:3])

    acc = jnp.zeros((TOK_TILE, D_MODEL), F32)
    ahead = FF_AHEAD
    for c in range(min(ahead, n_chunks)):
        up_project(c)
    for c in range(n_chunks):
        if c + ahead < n_chunks:
            up_project(c + ahead)
        a, val = conv(c, 0), conv(c, D_FF)
        act = (a * _sigmoid(a) * val).astype(BF16)
        acc += jnp.dot(act, wd_ref[c * FF_CHUNK:(c + 1) * FF_CHUNK, :], preferred_element_type=F32)
    o_ref[0] = x_ref[0] + gate2 * acc


def _conv_ffn(xm, h2, mod_l, w_up, conv_w, conv_b, w_down, n_batch, Lc, tile_off, n_tiles_total):
    B, T, D = xm.shape
    nt = T // TOK_TILE
    r = TOK_TILE // BF16_ROWS
    last = T // BF16_ROWS - 1
    full = lambda a: pl.BlockSpec(a.shape, lambda b, t: (0,) * a.ndim)
    tok = pl.BlockSpec((1, TOK_TILE, D), lambda b, t: (b, t, 0))
    halo_prev = pl.BlockSpec((1, BF16_ROWS, D), lambda b, t: (b, jnp.maximum(t * r - 1, 0), 0))
    halo_next = pl.BlockSpec((1, BF16_ROWS, D), lambda b, t: (b, jnp.minimum((t + 1) * r, last), 0))
    return pl.pallas_call(
        functools.partial(_ffn_kernel, n_batch=n_batch, n_ctx_tiles=Lc // TOK_TILE, tile_off=tile_off,
                          n_tiles_total=n_tiles_total),
        grid=(B, nt),
        in_specs=[tok, tok, halo_prev, halo_next, full(mod_l), full(w_up), full(conv_w), full(conv_b),
                  full(w_down)],
        out_specs=tok,
        out_shape=jax.ShapeDtypeStruct((B, T, D), F32),
        scratch_shapes=[pltpu.VMEM((TOK_TILE + 2 * BF16_ROWS, 2 * D_FF), F32)],
        compiler_params=_cparams("parallel", "arbitrary"),
        name="conv_ffn",
    )(xm, h2, h2, h2, mod_l, w_up, conv_w, conv_b, w_down)


def _with_ones_column(v):
    pad = jnp.zeros(v.shape[:-1] + (LANES - v.shape[-1],), v.dtype).at[..., 0].set(1.0)
    return jnp.concatenate([v, pad], axis=-1)


def kernel(x, c, ctx, c_ctx, w_mod, b_mod, norm1_g, norm2_g, w_in, win_qnorm_g, win_knorm_g, win_sink,
           hg_lower, hg_onorm_g, diff_qnorm_g, diff_knorm_g, diff_lambda, diff_onorm_g, w_out,
           w_up, conv_w, conv_b, w_down):
    B, L, D = x.shape
    Lc = ctx.shape[1]
    depth = w_mod.shape[0]
    Lt = Lc + L
    assert D == D_MODEL and L % GRID_W == 0
    assert Lc % TOK_TILE == 0 and L % TOK_TILE == 0 and TOK_TILE % HG_BLOCK == 0 and TOK_TILE % ATT_BLOCK == 0
    assert B + 1 <= SUBLANES

    vecs = jnp.zeros((SUBLANES, D), F32).at[:B].set(c.astype(F32)).at[B].set(c_ctx.astype(F32))
    mod = _adaln(vecs, w_mod.astype(F32), b_mod.astype(F32))
    lower, lam_raw = _prep_params(hg_lower, diff_lambda)

    tabs = _rope_tables(L, Lc, WIN_HEAD_DIM) + _rope_tables(L, Lc, DIFF_QK_DIM)
    g64 = _block_diag_ones(256, 64)
    g32 = _block_diag_ones(256, 32)
    row = lambda v, reps: jnp.tile(v.astype(F32), reps).reshape(1, -1)

    xc = jnp.concatenate([ctx.astype(F32), x.astype(F32)], axis=1)
    n_tiles_total = Lt // TOK_TILE
    nct = Lc // TOK_TILE

    for l in range(depth):
        need_ctx = l < depth - 1
        lam_init = 0.8 - 0.6 * math.exp(-0.3 * l)
        gains = (row(win_qnorm_g[l], WIN_HEADS), row(win_knorm_g[l], WIN_KV_HEADS),
                 row(diff_qnorm_g[l], 2 * DIFF_HEADS), row(diff_knorm_g[l], 2 * DIFF_HEADS))
        (qa, ka, va, hqf, hlff, hkf, hqb, hlfb, hkb, hi, hgate, qd, kd, vd) = _inproj(
            xc, mod[l], norm1_g[l].reshape(1, D).astype(F32), w_in[l].astype(BF16), tabs, gains, lower[l],
            g64, g32, B, Lc)

        qa_h = qa.reshape(B, Lt, WIN_HEADS, WIN_HEAD_DIM).transpose(0, 2, 1, 3)
        kaT = ka.reshape(B, Lt, WIN_KV_HEADS, WIN_HEAD_DIM).transpose(0, 2, 3, 1)
        va_h = _with_ones_column(va.reshape(B, Lt, WIN_KV_HEADS, WIN_HEAD_DIM).transpose(0, 2, 1, 3))
        sink = win_sink[l].astype(F32) * LOG2E
        oa_lat, oa_ctx = _win_attention(sink, qa_h, kaT, va_h, L, Lc, need_ctx)

        hiT = hi.transpose(0, 2, 1)
        of, ob = _hgrn_scan(hqf, hlff, hkf, hqb, hlfb, hkb, hi, hiT, g64, Lc)

        qdT = qd.transpose(0, 2, 1)
        vdT = vd.reshape(B, Lt, DIFF_HEADS, DIFF_V_DIM).transpose(0, 2, 3, 1)
        fill = jnp.zeros((B, DIFF_HEADS, DIFF_VT_ROWS - DIFF_V_DIM, Lt), BF16).at[:, :, 0].set(1.0)
        vdT_aug = jnp.concatenate([vdT, fill], axis=2)
        lam = lam_raw[l, 0, :1] + lam_init
        dog = diff_onorm_g[l].reshape(DIFF_V_DIM, 1).astype(F32)
        kf = kd.astype(F32).reshape(B, Lt, 2 * DIFF_HEADS, DIFF_QK_DIM)
        k_norm = jnp.sqrt(jnp.max(jnp.sum(kf * kf, axis=-1), axis=1)).reshape(B, 2 * DIFF_HEADS, 1)
        od_lat = _diff_attention(lam, qdT, kd, vdT_aug, k_norm, dog, Lc, L, Lt, 1.0 - lam_init)

        def token_major(o):
            return o.transpose(0, 2, 1, 3).reshape(B, o.shape[2], -1)

        def token_major_t(oT):
            return oT.transpose(0, 3, 1, 2).reshape(B, oT.shape[3], -1)

        if need_ctx:
            od_ctx = _diff_attention(lam, qdT, kd, vdT_aug, k_norm, dog, 0, Lc, Lc, 1.0 - lam_init)
            oa = jnp.concatenate([token_major(oa_ctx), token_major(oa_lat)], axis=1)
            od = jnp.concatenate([token_major_t(od_ctx), token_major_t(od_lat)], axis=1)
            tile_off, mix_off = 0, 0
        else:
            oa, od = token_major(oa_lat), token_major_t(od_lat)
            tile_off, mix_off = nct, nct

        xm, h2 = _outproj(xc, oa, of, ob, hgate, od, mod[l], w_out[l].astype(BF16),
                          row(hg_onorm_g[l], HG_HEADS), norm2_g[l].reshape(1, D).astype(F32), g64,
                          B, Lc, tile_off, mix_off)
        xc = _conv_ffn(xm, h2, mod[l], w_up[l].astype(BF16), conv_w[l].astype(F32),
                       conv_b[l].reshape(1, -1).astype(F32), w_down[l].astype(BF16),
                       B, Lc, tile_off, n_tiles_total)

    return xc
```

```python
import functools
import math

import numpy as np
import jax
import jax.numpy as jnp
from jax import lax
from jax.experimental import pallas as pl
from jax.experimental.pallas import tpu as pltpu

F32 = jnp.float32
BF16 = jnp.bfloat16

D_MODEL = 1024
GRID_W = 64
WIN_HEADS, WIN_KV_HEADS, WIN_HEAD_DIM = 8, 2, 64
WINDOW = 128
HG_HEADS, HG_KEY_DIM, HG_VAL_DIM = 4, 64, 64
DIFF_HEADS, DIFF_QK_DIM, DIFF_V_DIM = 4, 32, 64
D_FF = 11 * D_MODEL // 4
ROPE_BASE = 10000.0
EPS = 1e-6
LOG2E = math.log2(math.e)

_SPLITS = (512, 128, 128, 256, 256, 256, 256, 256, 256, 256, 256, 256)
_OFF = np.concatenate([[0], np.cumsum(_SPLITS)]).tolist()
IN_WIDTH = _OFF[-1]
(C_AQ, C_AK, C_AV, C_BQF, C_BFF, C_BQB, C_BFB, C_BI, C_BG, C_CQ, C_CK, C_CV) = _OFF[:-1]

LANES = 128
SUBLANES = 8
BF16_ROWS = 16
VMEM_LIMIT = 48 * 1024 * 1024

TOK_TILE = 256
ATT_BLOCK = 128
DIFF_TQ = 256
DIFF_MAX_SHIFT = 60.0
DIFF_MAX_KEY_BLOCK = 2816
DIFF_VT_ROWS = 80
HG_BLOCK = 256
HG_GROUP = 128
HG_SUB = 16
FF_CHUNK = 256
FF_AHEAD = 5
INPROJ_AHEAD = 512
NEG = -1e30


def _cparams(*sem):
    return pltpu.CompilerParams(dimension_semantics=sem, vmem_limit_bytes=VMEM_LIMIT)


def _sigmoid(x):
    return 1.0 / (1.0 + jnp.exp(-x))


def _split3(x):
    hi = x.astype(BF16)
    r = x - hi.astype(F32)
    mid = r.astype(BF16)
    lo = (r - mid.astype(F32)).astype(BF16)
    return hi, mid, lo


def _group_sumsq(x, g):
    sq = x * x
    hi = sq.astype(BF16)
    lo = (sq - hi.astype(F32)).astype(BF16)
    return jnp.dot(hi, g, preferred_element_type=F32) + jnp.dot(lo, g, preferred_element_type=F32)


def _block_diag_ones(width, group):
    i = np.arange(width)
    return jnp.asarray((i[:, None] // group) == (i[None, :] // group), dtype=BF16)


def _prep_kernel(hl_ref, dl_ref, lb_ref, lam_ref, *, depth):
    raw = [hl_ref[l] for l in range(depth)]
    mx = raw[0]
    for l in range(1, depth):
        mx = jnp.maximum(mx, raw[l])
    ex = [jnp.exp(r - mx) for r in raw]
    tot = ex[0]
    for l in range(1, depth):
        tot = tot + ex[l]
    p = [e / tot for e in ex]
    run = p[0]
    lb_ref[0] = run - p[0]
    for l in range(1, depth):
        run = run + p[l]
        lb_ref[l] = run - p[0]
    for l in range(depth):
        d = dl_ref[l]
        a = jnp.sum(d[0:1] * d[1:2], axis=-1, keepdims=True)
        c = jnp.sum(d[2:3] * d[3:4], axis=-1, keepdims=True)
        lam_ref[l] = jnp.broadcast_to(jnp.exp(a) - jnp.exp(c), (1, LANES))


def _prep_params(hg_lower, diff_lambda):
    depth = hg_lower.shape[0]
    return pl.pallas_call(
        functools.partial(_prep_kernel, depth=depth),
        out_shape=(jax.ShapeDtypeStruct(hg_lower.shape, F32),
                   jax.ShapeDtypeStruct((depth, 1, LANES), F32)),
        name="prep_params",
    )(hg_lower.astype(F32), diff_lambda.astype(F32))


def _adaln_kernel(v_ref, w_ref, b_ref, o_ref):
    v = v_ref[...]
    a = v * _sigmoid(v)
    a_hi, a_mid, _ = _split3(a)
    w = w_ref[0]
    w_hi, w_mid, _ = _split3(w)
    acc = jnp.dot(a_hi, w_hi, preferred_element_type=F32)
    acc += jnp.dot(a_mid, w_hi, preferred_element_type=F32)
    acc += jnp.dot(a_hi, w_mid, preferred_element_type=F32)
    o_ref[0] = acc + b_ref[0]


def _adaln(vecs, w_mod, b_mod):
    depth, d, n6 = w_mod.shape
    rows = vecs.shape[0]
    cb = 768
    assert n6 % cb == 0
    return pl.pallas_call(
        _adaln_kernel,
        grid=(depth, n6 // cb),
        in_specs=[pl.BlockSpec((rows, d), lambda l, j: (0, 0)),
                  pl.BlockSpec((1, d, cb), lambda l, j: (l, 0, j)),
                  pl.BlockSpec((1, 1, cb), lambda l, j: (l, 0, j))],
        out_specs=pl.BlockSpec((1, rows, cb), lambda l, j: (l, 0, j)),
        out_shape=jax.ShapeDtypeStruct((depth, rows, n6), F32),
        compiler_params=_cparams("arbitrary", "arbitrary"),
        name="adaln",
    )(vecs, w_mod, b_mod.reshape(depth, 1, n6))


def _rope_tables(L, Lc, dim):
    rows = L // GRID_W
    row = jnp.repeat(jnp.arange(rows, dtype=F32), GRID_W)
    col = jnp.tile(jnp.arange(GRID_W, dtype=F32), rows)
    axis_dim = dim // 2
    n = axis_dim // 2
    inv = jnp.power(ROPE_BASE, -jnp.arange(n, dtype=F32) * 2.0 / axis_dim)
    ar = row[:, None] * inv[None, :]
    ac = col[:, None] * inv[None, :]
    cos = jnp.concatenate([jnp.cos(ar), jnp.cos(ar), jnp.cos(ac), jnp.cos(ac)], axis=-1)
    sin = jnp.concatenate([-jnp.sin(ar), jnp.sin(ar), -jnp.sin(ac), jnp.sin(ac)], axis=-1)
    cos = jnp.concatenate([jnp.ones((Lc, dim), F32), cos], axis=0)
    sin = jnp.concatenate([jnp.zeros((Lc, dim), F32), sin], axis=0)
    rep = LANES // dim
    return jnp.tile(cos, (1, rep)), jnp.tile(sin, (1, rep))


def _rope(x, cos, sin, quarter):
    lane = lax.broadcasted_iota(jnp.int32, x.shape, 1)
    first = (lane % (2 * quarter)) < quarter
    partner = jnp.where(first, pltpu.roll(x, LANES - quarter, 1), pltpu.roll(x, quarter, 1))
    return x * cos + partner * sin


def _inproj_kernel(x_ref, mod_ref, n1g_ref, w_ref, cosa_ref, sina_ref, cosd_ref, sind_ref,
                   gqa_ref, gka_ref, gqd_ref, gkd_ref, lb_ref, g64_ref, g32_ref,
                   qa_ref, ka_ref, va_ref, hqf_ref, hlff_ref, hkf_ref, hqb_ref, hlfb_ref, hkb_ref,
                   hi_ref, hg_ref, qd_ref, kd_ref, vd_ref, p_ref, *, n_batch, n_ctx_tiles):
    b = pl.program_id(0)
    t = pl.program_id(1)
    mrow = jnp.where(t < n_ctx_tiles, n_batch, b)
    m = mod_ref[pl.ds(mrow, 1), :]
    shift, scale = m[:, 0:D_MODEL], m[:, D_MODEL:2 * D_MODEL]

    x = x_ref[0]
    ms = jnp.mean(x * x, axis=-1, keepdims=True)
    h = (x * lax.rsqrt(ms + EPS) * n1g_ref[...]) * (1.0 + scale) + shift
    h = h.astype(BF16)

    edges = sorted(set(_OFF) | {C_AQ + 256})
    issued = [0]

    def proj(c0, width):
        target = min(c0 + width + INPROJ_AHEAD, IN_WIDTH)
        while issued[0] < target:
            e0 = issued[0]
            e1 = min(e for e in edges if e > e0)
            p_ref[:, e0:e1] = jnp.dot(h, w_ref[:, e0:e1], preferred_element_type=F32)
            issued[0] = e1
        return p_ref[:, c0:c0 + width]

    g64 = g64_ref[...]
    g32 = g32_ref[...]
    cosa, sina = cosa_ref[...], sina_ref[...]
    cosd, sind = cosd_ref[...], sind_ref[...]

    for c in range(2):
        p = proj(C_AQ + 256 * c, 256)
        qn = p * lax.rsqrt(_group_sumsq(p, g64) * (1.0 / WIN_HEAD_DIM) + EPS)
        qn = qn * (gqa_ref[:, 256 * c:256 * (c + 1)] * (WIN_HEAD_DIM ** -0.5 * LOG2E))
        for j in range(2):
            r = _rope(qn[:, LANES * j:LANES * (j + 1)], cosa, sina, WIN_HEAD_DIM // 4)
            qa_ref[0, :, 256 * c + LANES * j:256 * c + LANES * (j + 1)] = r.astype(BF16)
    p = proj(C_AK, 128)
    kn = p * lax.rsqrt(_group_sumsq(p, g64[:LANES, :LANES]) * (1.0 / WIN_HEAD_DIM) + EPS) * gka_ref[...]
    ka_ref[0] = _rope(kn, cosa, sina, WIN_HEAD_DIM // 4).astype(BF16)
    va_ref[0] = proj(C_AV, 128).astype(BF16)

    for d, (cq, cf, q_ref, lf_ref, k_ref) in enumerate(((C_BQF, C_BFF, hqf_ref, hlff_ref, hkf_ref),
                                                        (C_BQB, C_BFB, hqb_ref, hlfb_ref, hkb_ref))):
        q_ref[0] = proj(cq, 256)
        lb = lb_ref[d:d + 1, :]
        f = lb + (1.0 - lb) * _sigmoid(proj(cf, 256))
        lf_ref[0] = jnp.log(f)
        k_ref[0] = 1.0 - f
    hi_ref[0] = proj(C_BI, 256).astype(BF16)
    g = proj(C_BG, 256)
    hg_ref[0] = g * _sigmoid(g)

    for c0, gain_ref, o_ref, mult in ((C_CQ, gqd_ref, qd_ref, DIFF_QK_DIM ** -0.5 * LOG2E),
                                      (C_CK, gkd_ref, kd_ref, 1.0)):
        p = proj(c0, 256)
        pn = p * lax.rsqrt(_group_sumsq(p, g32) * (1.0 / DIFF_QK_DIM) + EPS) * (gain_ref[...] * mult)
        for j in range(2):
            r = _rope(pn[:, LANES * j:LANES * (j + 1)], cosd, sind, DIFF_QK_DIM // 4)
            o_ref[0, :, LANES * j:LANES * (j + 1)] = r.astype(BF16)
    vd_ref[0] = proj(C_CV, 256).astype(BF16)


def _inproj(xc, mod_l, n1g, w_in, tabs, gains, lb_l, g64, g32, n_batch, Lc):
    B, Lt, D = xc.shape
    nt = Lt // TOK_TILE
    cosa, sina, cosd, sind = tabs
    gqa, gka, gqd, gkd = gains
    full = lambda a: pl.BlockSpec(a.shape, lambda b, t: (0,) * a.ndim)
    tab = pl.BlockSpec((TOK_TILE, LANES), lambda b, t: (t, 0))
    tok = lambda w: pl.BlockSpec((1, TOK_TILE, w), lambda b, t: (b, t, 0))
    widths = (512, 128, 128, 256, 256, 256, 256, 256, 256, 256, 256, 256, 256, 256)
    dtypes = (BF16, BF16, BF16, F32, F32, F32, F32, F32, F32, BF16, F32, BF16, BF16, BF16)
    return pl.pallas_call(
        functools.partial(_inproj_kernel, n_batch=n_batch, n_ctx_tiles=Lc // TOK_TILE),
        grid=(B, nt),
        in_specs=[tok(D), full(mod_l), full(n1g), full(w_in), tab, tab, tab, tab,
                  full(gqa), full(gka), full(gqd), full(gkd), full(lb_l), full(g64), full(g32)],
        out_specs=[tok(w) for w in widths],
        out_shape=[jax.ShapeDtypeStruct((B, Lt, w), dt) for w, dt in zip(widths, dtypes)],
        scratch_shapes=[pltpu.VMEM((TOK_TILE, IN_WIDTH), F32)],
        compiler_params=_cparams("parallel", "arbitrary"),
        name="inproj",
    )(xc, mod_l, n1g, w_in, cosa, sina, cosd, sind, gqa, gka, gqd, gkd, lb_l, g64, g32)


def _gqa_softmax_pv(s_blocks, v_blocks, sinks):
    rows = s_blocks[0].shape[0]
    per_head = rows // len(sinks)
    rid = lax.broadcasted_iota(jnp.int32, (rows, 1), 0)
    sink = jnp.full((rows, 1), sinks[-1], F32)
    for g in range(len(sinks) - 2, -1, -1):
        sink = jnp.where(rid < (g + 1) * per_head, sinks[g], sink)
    s = jnp.concatenate(s_blocks, axis=1)
    m = jnp.maximum(jnp.max(s, axis=-1, keepdims=True), sink)
    e = jnp.exp2(s - m).astype(BF16)
    pv = jnp.dot(e, jnp.concatenate(v_blocks, axis=0), preferred_element_type=F32)
    den = pv[:, WIN_HEAD_DIM:WIN_HEAD_DIM + 1] + jnp.exp2(sink - m)
    return pv[:, :WIN_HEAD_DIM] / den


def _win_attn_kernel(sink_ref, q_ref, kp_ref, kc_ref, kn_ref, kx_ref, vp_ref, vc_ref, vn_ref, vx_ref, o_ref):
    j = pl.program_id(1)
    nb = pl.num_programs(1)
    group = WIN_HEADS // WIN_KV_HEADS
    rows = group * ATT_BLOCK
    t = lax.broadcasted_iota(jnp.int32, (rows, ATT_BLOCK), 0) % ATT_BLOCK
    s = lax.broadcasted_iota(jnp.int32, (rows, ATT_BLOCK), 1)
    mask_prev = (s >= t) & (j > 0)
    mask_next = (s <= t) & (j < nb - 1)
    def scores(hk):
        q = q_ref[0, hk * group:(hk + 1) * group].reshape(rows, WIN_HEAD_DIM)
        dot = lambda k_ref: jnp.dot(q, k_ref[0, hk], preferred_element_type=F32)
        return [jnp.where(mask_prev, dot(kp_ref), NEG), dot(kc_ref),
                jnp.where(mask_next, dot(kn_ref), NEG), dot(kx_ref)]

    blocks_all = [scores(hk) for hk in range(WIN_KV_HEADS)]
    for hk in range(WIN_KV_HEADS):
        o = _gqa_softmax_pv(blocks_all[hk], [vp_ref[0, hk], vc_ref[0, hk], vn_ref[0, hk], vx_ref[0, hk]],
                            [sink_ref[hk * group + g] for g in range(group)])
        o_ref[0, hk * group:(hk + 1) * group] = o.reshape(group, ATT_BLOCK, WIN_HEAD_DIM).astype(BF16)


def _ctx_attn_kernel(sink_ref, q_ref, kx_ref, vx_ref, o_ref):
    group = WIN_HEADS // WIN_KV_HEADS
    n = q_ref.shape[2]
    for hk in range(WIN_KV_HEADS):
        q = q_ref[0, hk * group:(hk + 1) * group].reshape(group * n, WIN_HEAD_DIM)
        s_ctx = jnp.dot(q, kx_ref[0, hk], preferred_element_type=F32)
        o = _gqa_softmax_pv([s_ctx], [vx_ref[0, hk]], [sink_ref[hk * group + g] for g in range(group)])
        o_ref[0, hk * group:(hk + 1) * group] = o.reshape(group, n, WIN_HEAD_DIM).astype(BF16)


def _win_attention(sink, q_h, kT, v_h, L, Lc, need_ctx):
    B = q_h.shape[0]
    nb = L // ATT_BLOCK
    off = Lc // ATT_BLOCK
    dh = WIN_HEAD_DIM
    smem = pl.BlockSpec(memory_space=pltpu.SMEM)
    kblk = lambda f: pl.BlockSpec((1, WIN_KV_HEADS, dh, ATT_BLOCK), lambda b, j: (b, 0, 0, off + f(j)))
    vblk = lambda f: pl.BlockSpec((1, WIN_KV_HEADS, ATT_BLOCK, LANES), lambda b, j: (b, 0, off + f(j), 0))
    prev = lambda j: jnp.maximum(j - 1, 0)
    cur = lambda j: j
    nxt = lambda j: jnp.minimum(j + 1, nb - 1)
    kctx = pl.BlockSpec((1, WIN_KV_HEADS, dh, Lc), lambda b, j: (b, 0, 0, 0))
    vctx = pl.BlockSpec((1, WIN_KV_HEADS, Lc, LANES), lambda b, j: (b, 0, 0, 0))
    o_lat = pl.pallas_call(
        _win_attn_kernel,
        grid=(B, nb),
        in_specs=[smem, pl.BlockSpec((1, WIN_HEADS, ATT_BLOCK, dh), lambda b, j: (b, 0, off + j, 0)),
                  kblk(prev), kblk(cur), kblk(nxt), kctx, vblk(prev), vblk(cur), vblk(nxt), vctx],
        out_specs=pl.BlockSpec((1, WIN_HEADS, ATT_BLOCK, dh), lambda b, j: (b, 0, j, 0)),
        out_shape=jax.ShapeDtypeStruct((B, WIN_HEADS, L, dh), BF16),
        compiler_params=_cparams("parallel", "arbitrary"),
        name="win_attn",
    )(sink, q_h, kT, kT, kT, kT, v_h, v_h, v_h, v_h)
    if not need_ctx:
        return o_lat, None
    o_ctx = pl.pallas_call(
        _ctx_attn_kernel,
        grid=(B,),
        in_specs=[smem, pl.BlockSpec((1, WIN_HEADS, Lc, dh), lambda b: (b, 0, 0, 0)),
                  pl.BlockSpec((1, WIN_KV_HEADS, dh, Lc), lambda b: (b, 0, 0, 0)),
                  pl.BlockSpec((1, WIN_KV_HEADS, Lc, LANES), lambda b: (b, 0, 0, 0))],
        out_specs=pl.BlockSpec((1, WIN_HEADS, Lc, dh), lambda b: (b, 0, 0, 0)),
        out_shape=jax.ShapeDtypeStruct((B, WIN_HEADS, Lc, dh), BF16),
        compiler_params=_cparams("parallel"),
        name="ctx_attn",
    )(sink, q_h, kT, v_h)
    return o_lat, o_ctx


def _hgrn_direction(q_ref, lf_ref, k_ref, v_ref, vT_ref, g64_ref, o_ref, st_ref, *, reverse):
    @pl.when(pl.program_id(1) == 0)
    def _():
        st_ref[...] = jnp.zeros_like(st_ref)

    width = HG_HEADS * HG_KEY_DIM
    n_pairs = width // LANES
    n_groups = HG_BLOCK // HG_GROUP
    n_sub = HG_GROUP // HG_SUB
    half = HG_SUB // 2
    ri = lax.broadcasted_iota(jnp.int32, (HG_GROUP, HG_GROUP), 0)
    ci = lax.broadcasted_iota(jnp.int32, (HG_GROUP, HG_GROUP), 1)
    in_sub = (ri // HG_SUB) == (ci // HG_SUB)
    tri = (in_sub & ((ci >= ri) if reverse else (ci <= ri))).astype(BF16)
    rows_full = lax.broadcasted_iota(jnp.int32, (HG_SUB, width), 0)
    rows_half = lax.broadcasted_iota(jnp.int32, (half, width), 0) + (0 if reverse else half)
    same_head = (lax.broadcasted_iota(jnp.int32, (LANES, LANES), 0) // HG_VAL_DIM
                 == lax.broadcasted_iota(jnp.int32, (LANES, LANES), 1) // HG_KEY_DIM)
    g64 = g64_ref[...]
    end_row = 0 if reverse else HG_SUB - 1
    half_lo = 0 if reverse else half

    def is_full(s):
        return (s >= half) if reverse else (s < half)

    def group(gi):
        g = (n_groups - 1 - gi) if reverse else gi
        g0 = pl.multiple_of(g * HG_GROUP, HG_GROUP)
        lf = lf_ref[0, pl.ds(g0, HG_GROUP), :]
        q = q_ref[0, pl.ds(g0, HG_GROUP), :]
        k = k_ref[0, pl.ds(g0, HG_GROUP), :]
        vf = v_ref[0, pl.ds(g0, HG_GROUP), :].astype(F32)
        vT = vT_ref[0, :, pl.ds(g0, HG_GROUP)]

        hi, mid, lo = _split3(lf)
        bcum = (jnp.dot(tri, hi, preferred_element_type=F32) + jnp.dot(tri, mid, preferred_element_type=F32)
                + jnp.dot(tri, lo, preferred_element_type=F32))
        qt = (q * jnp.exp(bcum)).astype(BF16)

        ends = [bcum[c * HG_SUB + end_row:c * HG_SUB + end_row + 1] for c in range(n_sub)]
        bend = jnp.concatenate([jnp.broadcast_to(e, (HG_SUB, width)) for e in ends], axis=0)
        khat = (k * jnp.exp(bend - bcum)).astype(BF16)

        def state_update(c, p):
            r0 = c * HG_SUB
            pieces = [khat[r0:r0 + HG_SUB, p * LANES:(p + 1) * LANES]]
            if r0:
                pieces.insert(0, jnp.zeros((r0, LANES), BF16))
            if r0 + HG_SUB < HG_GROUP:
                pieces.append(jnp.zeros((HG_GROUP - r0 - HG_SUB, LANES), BF16))
            return jnp.dot(vT[p * LANES:(p + 1) * LANES, :], jnp.concatenate(pieces, axis=0),
                           preferred_element_type=F32)

        upds = [[state_update(c, p) for p in range(n_pairs)] for c in range(n_sub)]
        sts = [st_ref[p] for p in range(n_pairs)]
        inter = [None] * n_sub
        for c in (range(n_sub - 1, -1, -1) if reverse else range(n_sub)):
            r0 = c * HG_SUB
            dec = jnp.exp(ends[c])
            o_parts = []
            for p in range(n_pairs):
                ls = slice(p * LANES, (p + 1) * LANES)
                st_b = jnp.where(same_head, sts[p], 0.0).astype(BF16)
                o_parts.append(lax.dot_general(qt[r0:r0 + HG_SUB, ls], st_b, (((1,), (1,)), ((), ())),
                                               preferred_element_type=F32))
                sts[p] = sts[p] * dec[:, ls] + upds[c][p]
            inter[c] = jnp.concatenate(o_parts, axis=1)
        for p in range(n_pairs):
            st_ref[p] = sts[p]

        parts = []
        for c in range(n_sub):
            r0 = c * HG_SUB
            bc, qc, kc = bcum[r0:r0 + HG_SUB], q[r0:r0 + HG_SUB], k[r0:r0 + HG_SUB]
            for s in range(HG_SUB):
                if is_full(s):
                    rows, bt, qq = rows_full, bc, qc
                else:
                    rows, bt, qq = rows_half, bc[half_lo:half_lo + half], qc[half_lo:half_lo + half]
                valid = (rows <= s) if reverse else (rows >= s)
                parts.append(jnp.where(valid, qq * jnp.exp(bt - bc[s:s + 1]) * kc[s:s + 1], 0.0))
        a = jnp.dot(jnp.concatenate(parts, axis=0).astype(BF16), g64, preferred_element_type=F32)
        intra = []
        off = 0
        for c in range(n_sub):
            r0 = c * HG_SUB
            o_full = jnp.zeros((HG_SUB, width), F32)
            o_half = jnp.zeros((half, width), F32)
            for s in range(HG_SUB):
                vs = vf[r0 + s:r0 + s + 1]
                if is_full(s):
                    o_full = o_full + a[off:off + HG_SUB] * vs
                    off += HG_SUB
                else:
                    o_half = o_half + a[off:off + half] * vs
                    off += half
            pad = jnp.zeros((half, width), F32)
            intra.append(o_full + jnp.concatenate([o_half, pad] if reverse else [pad, o_half], axis=0))

        for c in range(n_sub):
            o_ref[0, pl.ds(g0 + c * HG_SUB, HG_SUB), :] = intra[c] + inter[c]

    return group


def _hgrn_kernel(qf_ref, lff_ref, kf_ref, vf_ref, vTf_ref, qb_ref, lfb_ref, kb_ref, vb_ref, vTb_ref, g64_ref,
                 of_ref, ob_ref, stf_ref, stb_ref):
    fwd = _hgrn_direction(qf_ref, lff_ref, kf_ref, vf_ref, vTf_ref, g64_ref, of_ref, stf_ref, reverse=False)
    bwd = _hgrn_direction(qb_ref, lfb_ref, kb_ref, vb_ref, vTb_ref, g64_ref, ob_ref, stb_ref, reverse=True)

    def body(gi, carry):
        fwd(gi)
        bwd(gi)
        return carry

    lax.fori_loop(0, HG_BLOCK // HG_GROUP, body, 0)


def _hgrn_scan(qf, lff, kf, qb, lfb, kb, v, vT, g64, Lc):
    B, Lt, W = qf.shape
    nblk = Lt // HG_BLOCK
    nctx = Lc // HG_BLOCK
    rev = lambda i: jnp.where(i < nctx, nctx - 1 - i, nctx + (nblk - 1 - i))
    fspec = pl.BlockSpec((1, HG_BLOCK, W), lambda b, i: (b, i, 0))
    fspec_t = pl.BlockSpec((1, W, HG_BLOCK), lambda b, i: (b, 0, i))
    bspec = pl.BlockSpec((1, HG_BLOCK, W), lambda b, i: (b, rev(i), 0))
    bspec_t = pl.BlockSpec((1, W, HG_BLOCK), lambda b, i: (b, 0, rev(i)))
    state = pltpu.VMEM((W // LANES, LANES, LANES), F32)
    return pl.pallas_call(
        _hgrn_kernel,
        grid=(B, nblk),
        in_specs=[fspec, fspec, fspec, fspec, fspec_t, bspec, bspec, bspec, bspec, bspec_t,
                  pl.BlockSpec(g64.shape, lambda b, i: (0, 0))],
        out_specs=[fspec, bspec],
        out_shape=[jax.ShapeDtypeStruct((B, Lt, W), F32)] * 2,
        scratch_shapes=[state, state],
        compiler_params=_cparams("parallel", "arbitrary"),
        name="hgrn",
    )(qf, lff, kf, v, vT, qb, lfb, kb, v, vT, g64)


def _diff_attn_kernel(lam_ref, qT_ref, k_ref, vT_ref, kn_ref, og_ref, o_ref, *, key_block, n_keys, post_scale):
    nkb = n_keys // key_block
    n_chain = 2 * DIFF_HEADS
    per_half = LANES // DIFF_QK_DIM
    band = lax.broadcasted_iota(jnp.int32, (LANES, DIFF_TQ), 0) // DIFF_QK_DIM
    ws = []
    for j in range(n_chain):
        half = qT_ref[0, (j // per_half) * LANES:(j // per_half + 1) * LANES, :]
        ws.append(jnp.where(band == j % per_half, half, jnp.zeros_like(half)))

    def scores(kb):
        k0 = pl.multiple_of(kb * key_block, key_block)
        kblks = [k_ref[0, pl.ds(k0, key_block), p * LANES:(p + 1) * LANES] for p in range(n_chain // per_half)]
        sTs = [jnp.dot(kblks[j // per_half], ws[j], preferred_element_type=F32) for j in range(n_chain)]
        vTs = [vT_ref[0, h, :, pl.ds(k0, key_block)] for h in range(DIFF_HEADS)]
        return sTs, vTs

    def pv(vTs, eTs):
        return [jnp.dot(vTs[j // 2], eTs[j], preferred_element_type=F32) for j in range(n_chain)]

    qf = qT_ref[0].astype(F32)
    qn = jnp.sqrt(jnp.sum((qf * qf).reshape(n_chain, DIFF_QK_DIM, DIFF_TQ), axis=1))
    bound = qn * kn_ref[0] * (1.0 + 2.0 ** -10) + 2.0 ** -10
    shift = [bound[j:j + 1] for j in range(n_chain)]
    acc0 = tuple(jnp.zeros((DIFF_VT_ROWS, DIFF_TQ), F32) for _ in range(n_chain))

    def fixed_shift():
        def body(kb, accs):
            sTs, vTs = scores(kb)
            pvs = pv(vTs, [jnp.exp2(sTs[j] - shift[j]).astype(BF16) for j in range(n_chain)])
            return tuple(accs[j] + pvs[j] for j in range(n_chain))
        return lax.fori_loop(0, nkb, body, acc0)

    def running_max():
        def body(kb, carry):
            ms, accs = carry
            sTs, vTs = scores(kb)
            m_new = [jnp.maximum(ms[j], jnp.max(sTs[j], axis=0, keepdims=True)) for j in range(n_chain)]
            pvs = pv(vTs, [jnp.exp2(sTs[j] - m_new[j]).astype(BF16) for j in range(n_chain)])
            accs = tuple(accs[j] * jnp.exp2(ms[j] - m_new[j]) + pvs[j] for j in range(n_chain))
            return tuple(m_new), accs
        m0 = tuple(jnp.full((1, DIFF_TQ), NEG, F32) for _ in range(n_chain))
        return lax.fori_loop(0, nkb, body, (m0, acc0))[1]

    accs = lax.cond(jnp.max(bound) <= DIFF_MAX_SHIFT, fixed_shift, running_max)
    lam = lam_ref[0]
    for h in range(DIFF_HEADS):
        a0, a1 = accs[2 * h], accs[2 * h + 1]
        o = (a0[:DIFF_V_DIM] / a0[DIFF_V_DIM:DIFF_V_DIM + 1]
             - lam * (a1[:DIFF_V_DIM] / a1[DIFF_V_DIM:DIFF_V_DIM + 1]))
        ms = jnp.mean(o * o, axis=0, keepdims=True)
        o_ref[0, h] = (o * lax.rsqrt(ms + EPS) * (og_ref[...] * post_scale)).astype(BF16)


def _pick_key_block(n):
    best = LANES
    for kb in range(LANES, min(n, DIFF_MAX_KEY_BLOCK) + 1, LANES):
        if n % kb == 0:
            best = kb
    return best


def _diff_attention(lam, qT, k, vT_aug, k_norm, og_col, q_off, n_q, n_keys, post_scale):
    B = qT.shape[0]
    width = 2 * DIFF_HEADS * DIFF_QK_DIM
    nq = n_q // DIFF_TQ
    qo = q_off // DIFF_TQ
    n_chain = 2 * DIFF_HEADS
    return pl.pallas_call(
        functools.partial(_diff_attn_kernel, key_block=_pick_key_block(n_keys), n_keys=n_keys,
                          post_scale=post_scale),
        grid=(B, nq),
        in_specs=[pl.BlockSpec(memory_space=pltpu.SMEM),
                  pl.BlockSpec((1, width, DIFF_TQ), lambda b, i: (b, 0, qo + i)),
                  pl.BlockSpec((1, n_keys, width), lambda b, i: (b, 0, 0)),
                  pl.BlockSpec((1, DIFF_HEADS, DIFF_VT_ROWS, n_keys), lambda b, i: (b, 0, 0, 0)),
                  pl.BlockSpec((1, n_chain, 1), lambda b, i: (b, 0, 0)),
                  pl.BlockSpec((DIFF_V_DIM, 1), lambda b, i: (0, 0))],
        out_specs=pl.BlockSpec((1, DIFF_HEADS, DIFF_V_DIM, DIFF_TQ), lambda b, i: (b, 0, 0, i)),
        out_shape=jax.ShapeDtypeStruct((B, DIFF_HEADS, DIFF_V_DIM, n_q), BF16),
        compiler_params=_cparams("parallel", "arbitrary"),
        name="diff_attn",
    )(lam, qT, k, vT_aug, k_norm, og_col)


def _outproj_kernel(x_ref, oa_ref, of_ref, ob_ref, gate_ref, od_ref, mod_ref, w_ref, hog_ref, n2g_ref, g64_ref,
                    xm_ref, h2_ref, *, n_batch, n_ctx_tiles, tile_off):
    b = pl.program_id(0)
    t = pl.program_id(1) + tile_off
    mrow = jnp.where(t < n_ctx_tiles, n_batch, b)
    m = mod_ref[pl.ds(mrow, 1), :]
    gate1 = m[:, 2 * D_MODEL:3 * D_MODEL]
    shift2, scale2 = m[:, 3 * D_MODEL:4 * D_MODEL], m[:, 4 * D_MODEL:5 * D_MODEL]

    o = of_ref[0] + ob_ref[0]
    ss = _group_sumsq(o, g64_ref[...])
    wa, wb = WIN_HEADS * WIN_HEAD_DIM, HG_HEADS * HG_VAL_DIM
    mix = jnp.dot(oa_ref[0], w_ref[0:wa, :], preferred_element_type=F32)
    mix += jnp.dot(od_ref[0], w_ref[wa + wb:, :], preferred_element_type=F32)
    on = o * lax.rsqrt(ss * (1.0 / HG_VAL_DIM) + EPS) * hog_ref[...]
    ob = (on * gate_ref[0]).astype(BF16)
    mix += jnp.dot(ob, w_ref[wa:wa + wb, :], preferred_element_type=F32)

    x = x_ref[0] + gate1 * mix
    xm_ref[0] = x
    ms = jnp.mean(x * x, axis=-1, keepdims=True)
    h = (x * lax.rsqrt(ms + EPS) * n2g_ref[...]) * (1.0 + scale2) + shift2
    h2_ref[0] = h.astype(BF16)


def _outproj(xc, oa, of, ob, gate, od, mod_l, w_out, hog, n2g, g64, n_batch, Lc, tile_off, mix_off):
    B, Lt, D = xc.shape
    nt = Lt // TOK_TILE - tile_off
    full = lambda a: pl.BlockSpec(a.shape, lambda b, t: (0,) * a.ndim)
    comb = lambda w: pl.BlockSpec((1, TOK_TILE, w), lambda b, t: (b, t + tile_off, 0))
    part = lambda w: pl.BlockSpec((1, TOK_TILE, w), lambda b, t: (b, t + tile_off - mix_off, 0))
    outs = pl.BlockSpec((1, TOK_TILE, D), lambda b, t: (b, t, 0))
    return pl.pallas_call(
        functools.partial(_outproj_kernel, n_batch=n_batch, n_ctx_tiles=Lc // TOK_TILE, tile_off=tile_off),
        grid=(B, nt),
        in_specs=[comb(D), part(oa.shape[-1]), comb(256), comb(256), comb(256), part(od.shape[-1]),
                  full(mod_l), full(w_out), full(hog), full(n2g), full(g64)],
        out_specs=[outs, outs],
        out_shape=[jax.ShapeDtypeStruct((B, nt * TOK_TILE, D), F32),
                   jax.ShapeDtypeStruct((B, nt * TOK_TILE, D), BF16)],
        compiler_params=_cparams("parallel", "arbitrary"),
        name="outproj",
    )(xc, oa, of, ob, gate, od, mod_l, w_out, hog, n2g, g64)


def _ffn_kernel(x_ref, h_ref, hp_ref, hn_ref, mod_ref, wu_ref, cw_ref, cb_ref, wd_ref, o_ref, u_ref,
                *, n_batch, n_ctx_tiles, tile_off, n_tiles_total):
    b = pl.program_id(0)
    t = pl.program_id(1) + tile_off
    mrow = jnp.where(t < n_ctx_tiles, n_batch, b)
    gate2 = mod_ref[pl.ds(mrow, 1), :][:, 5 * D_MODEL:6 * D_MODEL]

    prev_ok = jnp.logical_and(t != 0, t != n_ctx_tiles)
    next_ok = jnp.logical_and(t != n_ctx_tiles - 1, t != n_tiles_total - 1)
    hp = jnp.where(prev_ok, hp_ref[0], jnp.zeros_like(hp_ref[0]))
    hn = jnp.where(next_ok, hn_ref[0], jnp.zeros_like(hn_ref[0]))
    hcat = jnp.concatenate([hp, h_ref[0], hn], axis=0)

    lo = BF16_ROWS
    n_chunks = D_FF // FF_CHUNK

    def up_project(c):
        for base in (0, D_FF):
            cols = slice(base + c * FF_CHUNK, base + (c + 1) * FF_CHUNK)
            u_ref[:, cols] = jnp.dot(hcat, wu_ref[:, cols], preferred_element_type=F32)

    def conv(c, base):
        cols = slice(base + c * FF_CHUNK, base + (c + 1) * FF_CHUNK)
        w = cw_ref[:, cols]
        return (cb_ref[:, cols] + u_ref[lo - 1:lo - 1 + TOK_TILE, cols] * w[0:1]
                + u_ref[lo:lo + TOK_TILE, cols] * w[1:2] + u_ref[lo + 1:lo + 1 + TOK_TILE, cols] * w[2:3])

    acc = jnp.zeros((TOK_TILE, D_MODEL), F32)
    ahead = FF_AHEAD
    for c in range(min(ahead, n_chunks)):
        up_project(c)
    for c in range(n_chunks):
        if c + ahead < n_chunks:
            up_project(c + ahead)
        a, val = conv(c, 0), conv(c, D_FF)
        act = (a * _sigmoid(a) * val).astype(BF16)
        acc += jnp.dot(act, wd_ref[c * FF_CHUNK:(c + 1) * FF_CHUNK, :], preferred_element_type=F32)
    o_ref[0] = x_ref[0] + gate2 * acc


def _conv_ffn(xm, h2, mod_l, w_up, conv_w, conv_b, w_down, n_batch, Lc, tile_off, n_tiles_total):
    B, T, D = xm.shape
    nt = T // TOK_TILE
    r = TOK_TILE // BF16_ROWS
    last = T // BF16_ROWS - 1
    full = lambda a: pl.BlockSpec(a.shape, lambda b, t: (0,) * a.ndim)
    tok = pl.BlockSpec((1, TOK_TILE, D), lambda b, t: (b, t, 0))
    halo_prev = pl.BlockSpec((1, BF16_ROWS, D), lambda b, t: (b, jnp.maximum(t * r - 1, 0), 0))
    halo_next = pl.BlockSpec((1, BF16_ROWS, D), lambda b, t: (b, jnp.minimum((t + 1) * r, last), 0))
    return pl.pallas_call(
        functools.partial(_ffn_kernel, n_batch=n_batch, n_ctx_tiles=Lc // TOK_TILE, tile_off=tile_off,
                          n_tiles_total=n_tiles_total),
        grid=(B, nt),
        in_specs=[tok, tok, halo_prev, halo_next, full(mod_l), full(w_up), full(conv_w), full(conv_b),
                  full(w_down)],
        out_specs=tok,
        out_shape=jax.ShapeDtypeStruct((B, T, D), F32),
        scratch_shapes=[pltpu.VMEM((TOK_TILE + 2 * BF16_ROWS, 2 * D_FF), F32)],
        compiler_params=_cparams("parallel", "arbitrary"),
        name="conv_ffn",
    )(xm, h2, h2, h2, mod_l, w_up, conv_w, conv_b, w_down)


def _with_ones_column(v):
    pad = jnp.zeros(v.shape[:-1] + (LANES - v.shape[-1],), v.dtype).at[..., 0].set(1.0)
    return jnp.concatenate([v, pad], axis=-1)


def kernel(x, c, ctx, c_ctx, w_mod, b_mod, norm1_g, norm2_g, w_in, win_qnorm_g, win_knorm_g, win_sink,
           hg_lower, hg_onorm_g, diff_qnorm_g, diff_knorm_g, diff_lambda, diff_onorm_g, w_out,
           w_up, conv_w, conv_b, w_down):
    B, L, D = x.shape
    Lc = ctx.shape[1]
    depth = w_mod.shape[0]
    Lt = Lc + L
    assert D == D_MODEL and L % GRID_W == 0
    assert Lc % TOK_TILE == 0 and L % TOK_TILE == 0 and TOK_TILE % HG_BLOCK == 0 and TOK_TILE % ATT_BLOCK == 0
    assert B + 1 <= SUBLANES

    vecs = jnp.zeros((SUBLANES, D), F32).at[:B].set(c.astype(F32)).at[B].set(c_ctx.astype(F32))
    mod = _adaln(vecs, w_mod.astype(F32), b_mod.astype(F32))
    lower, lam_raw = _prep_params(hg_lower, diff_lambda)

    tabs = _rope_tables(L, Lc, WIN_HEAD_DIM) + _rope_tables(L, Lc, DIFF_QK_DIM)
    g64 = _block_diag_ones(256, 64)
    g32 = _block_diag_ones(256, 32)
    row = lambda v, reps: jnp.tile(v.astype(F32), reps).reshape(1, -1)

    xc = jnp.concatenate([ctx.astype(F32), x.astype(F32)], axis=1)
    n_tiles_total = Lt // TOK_TILE
    nct = Lc // TOK_TILE

    for l in range(depth):
        need_ctx = l < depth - 1
        lam_init = 0.8 - 0.6 * math.exp(-0.3 * l)
        gains = (row(win_qnorm_g[l], WIN_HEADS), row(win_knorm_g[l], WIN_KV_HEADS),
                 row(diff_qnorm_g[l], 2 * DIFF_HEADS), row(diff_knorm_g[l], 2 * DIFF_HEADS))
        (qa, ka, va, hqf, hlff, hkf, hqb, hlfb, hkb, hi, hgate, qd, kd, vd) = _inproj(
            xc, mod[l], norm1_g[l].reshape(1, D).astype(F32), w_in[l].astype(BF16), tabs, gains, lower[l],
            g64, g32, B, Lc)

        qa_h = qa.reshape(B, Lt, WIN_HEADS, WIN_HEAD_DIM).transpose(0, 2, 1, 3)
        kaT = ka.reshape(B, Lt, WIN_KV_HEADS, WIN_HEAD_DIM).transpose(0, 2, 3, 1)
        va_h = _with_ones_column(va.reshape(B, Lt, WIN_KV_HEADS, WIN_HEAD_DIM).transpose(0, 2, 1, 3))
        sink = win_sink[l].astype(F32) * LOG2E
        oa_lat, oa_ctx = _win_attention(sink, qa_h, kaT, va_h, L, Lc, need_ctx)

        hiT = hi.transpose(0, 2, 1)
        of, ob = _hgrn_scan(hqf, hlff, hkf, hqb, hlfb, hkb, hi, hiT, g64, Lc)

        qdT = qd.transpose(0, 2, 1)
        vdT = vd.reshape(B, Lt, DIFF_HEADS, DIFF_V_DIM).transpose(0, 2, 3, 1)
        fill = jnp.zeros((B, DIFF_HEADS, DIFF_VT_ROWS - DIFF_V_DIM, Lt), BF16).at[:, :, 0].set(1.0)
        vdT_aug = jnp.concatenate([vdT, fill], axis=2)
        lam = lam_raw[l, 0, :1] + lam_init
        dog = diff_onorm_g[l].reshape(DIFF_V_DIM, 1).astype(F32)
        kf = kd.astype(F32).reshape(B, Lt, 2 * DIFF_HEADS, DIFF_QK_DIM)
        k_norm = jnp.sqrt(jnp.max(jnp.sum(kf * kf, axis=-1), axis=1)).reshape(B, 2 * DIFF_HEADS, 1)
        od_lat = _diff_attention(lam, qdT, kd, vdT_aug, k_norm, dog, Lc, L, Lt, 1.0 - lam_init)

        def token_major(o):
            return o.transpose(0, 2, 1, 3).reshape(B, o.shape[2], -1)

        def token_major_t(oT):
            return oT.transpose(0, 3, 1, 2).reshape(B, oT.shape[3], -1)

        if need_ctx:
            od_ctx = _diff_attention(lam, qdT, kd, vdT_aug, k_norm, dog, 0, Lc, Lc, 1.0 - lam_init)
            oa = jnp.concatenate([token_major(oa_ctx), token_major(oa_lat)], axis=1)
            od = jnp.concatenate([token_major_t(od_ctx), token_major_t(od_lat)], axis=1)
            tile_off, mix_off = 0, 0
        else:
            oa, od = token_major(oa_lat), token_major_t(od_lat)
            tile_off, mix_off = nct, nct

        xm, h2 = _outproj(xc, oa, of, ob, hgate, od, mod[l], w_out[l].astype(BF16),
                          row(hg_onorm_g[l], HG_HEADS), norm2_g[l].reshape(1, D).astype(F32), g64,
                          B, Lc, tile_off, mix_off)
        xc = _conv_ffn(xm, h2, mod[l], w_up[l].astype(BF16), conv_w[l].astype(F32),
                       conv_b[l].reshape(1, -1).astype(F32), w_down[l].astype(BF16),
                       B, Lc, tile_off, n_tiles_total)

    return xc
```

```python
import functools
import math

import numpy as np
import jax
import jax.numpy as jnp
from jax import lax
from jax.experimental import pallas as pl
from jax.experimental.pallas import tpu as pltpu

F32 = jnp.float32
BF16 = jnp.bfloat16

D_MODEL = 1024
GRID_W = 64
WIN_HEADS, WIN_KV_HEADS, WIN_HEAD_DIM = 8, 2, 64
WINDOW = 128
HG_HEADS, HG_KEY_DIM, HG_VAL_DIM = 4, 64, 64
DIFF_HEADS, DIFF_QK_DIM, DIFF_V_DIM = 4, 32, 64
D_FF = 11 * D_MODEL // 4
ROPE_BASE = 10000.0
EPS = 1e-6
LOG2E = math.log2(math.e)

_SPLITS = (512, 128, 128, 256, 256, 256, 256, 256, 256, 256, 256, 256)
_OFF = np.concatenate([[0], np.cumsum(_SPLITS)]).tolist()
IN_WIDTH = _OFF[-1]
(C_AQ, C_AK, C_AV, C_BQF, C_BFF, C_BQB, C_BFB, C_BI, C_BG, C_CQ, C_CK, C_CV) = _OFF[:-1]

LANES = 128
SUBLANES = 8
BF16_ROWS = 16
VMEM_LIMIT = 48 * 1024 * 1024

TOK_TILE = 256
ATT_BLOCK = 128
DIFF_TQ = 256
DIFF_MAX_SHIFT = 60.0
DIFF_MAX_KEY_BLOCK = 2816
DIFF_VT_ROWS = 80
HG_BLOCK = 256
HG_GROUP = 128
HG_SUB = 16
FF_CHUNK = 256
FF_AHEAD = 5
INPROJ_AHEAD = 512
NEG = -1e30


def _cparams(*sem):
    return pltpu.CompilerParams(dimension_semantics=sem, vmem_limit_bytes=VMEM_LIMIT)


def _sigmoid(x):
    return 1.0 / (1.0 + jnp.exp(-x))


def _split3(x):
    hi = x.astype(BF16)
    r = x - hi.astype(F32)
    mid = r.astype(BF16)
    lo = (r - mid.astype(F32)).astype(BF16)
    return hi, mid, lo


def _group_sumsq(x, g):
    sq = x * x
    hi = sq.astype(BF16)
    lo = (sq - hi.astype(F32)).astype(BF16)
    return jnp.dot(hi, g, preferred_element_type=F32) + jnp.dot(lo, g, preferred_element_type=F32)


def _block_diag_ones(width, group):
    i = np.arange(width)
    return jnp.asarray((i[:, None] // group) == (i[None, :] // group), dtype=BF16)


def _prep_kernel(hl_ref, dl_ref, lb_ref, lam_ref, *, depth):
    raw = [hl_ref[l] for l in range(depth)]
    mx = raw[0]
    for l in range(1, depth):
        mx = jnp.maximum(mx, raw[l])
    ex = [jnp.exp(r - mx) for r in raw]
    tot = ex[0]
    for l in range(1, depth):
        tot = tot + ex[l]
    p = [e / tot for e in ex]
    run = p[0]
    lb_ref[0] = run - p[0]
    for l in range(1, depth):
        run = run + p[l]
        lb_ref[l] = run - p[0]
    for l in range(depth):
        d = dl_ref[l]
        a = jnp.sum(d[0:1] * d[1:2], axis=-1, keepdims=True)
        c = jnp.sum(d[2:3] * d[3:4], axis=-1, keepdims=True)
        lam_ref[l] = jnp.broadcast_to(jnp.exp(a) - jnp.exp(c), (1, LANES))


def _prep_params(hg_lower, diff_lambda):
    depth = hg_lower.shape[0]
    return pl.pallas_call(
        functools.partial(_prep_kernel, depth=depth),
        out_shape=(jax.ShapeDtypeStruct(hg_lower.shape, F32),
                   jax.ShapeDtypeStruct((depth, 1, LANES), F32)),
        name="prep_params",
    )(hg_lower.astype(F32), diff_lambda.astype(F32))


def _adaln_kernel(v_ref, w_ref, b_ref, o_ref):
    v = v_ref[...]
    a = v * _sigmoid(v)
    a_hi, a_mid, _ = _split3(a)
    w = w_ref[0]
    w_hi, w_mid, _ = _split3(w)
    acc = jnp.dot(a_hi, w_hi, preferred_element_type=F32)
    acc += jnp.dot(a_mid, w_hi, preferred_element_type=F32)
    acc += jnp.dot(a_hi, w_mid, preferred_element_type=F32)
    o_ref[0] = acc + b_ref[0]


def _adaln(vecs, w_mod, b_mod):
    depth, d, n6 = w_mod.shape
    rows = vecs.shape[0]
    cb = 768
    assert n6 % cb == 0
    return pl.pallas_call(
        _adaln_kernel,
        grid=(depth, n6 // cb),
        in_specs=[pl.BlockSpec((rows, d), lambda l, j: (0, 0)),
                  pl.BlockSpec((1, d, cb), lambda l, j: (l, 0, j)),
                  pl.BlockSpec((1, 1, cb), lambda l, j: (l, 0, j))],
        out_specs=pl.BlockSpec((1, rows, cb), lambda l, j: (l, 0, j)),
        out_shape=jax.ShapeDtypeStruct((depth, rows, n6), F32),
        compiler_params=_cparams("arbitrary", "arbitrary"),
        name="adaln",
    )(vecs, w_mod, b_mod.reshape(depth, 1, n6))


def _rope_tables(L, Lc, dim):
    rows = L // GRID_W
    row = jnp.repeat(jnp.arange(rows, dtype=F32), GRID_W)
    col = jnp.tile(jnp.arange(GRID_W, dtype=F32), rows)
    axis_dim = dim // 2
    n = axis_dim // 2
    inv = jnp.power(ROPE_BASE, -jnp.arange(n, dtype=F32) * 2.0 / axis_dim)
    ar = row[:, None] * inv[None, :]
    ac = col[:, None] * inv[None, :]
    cos = jnp.concatenate([jnp.cos(ar), jnp.cos(ar), jnp.cos(ac), jnp.cos(ac)], axis=-1)
    sin = jnp.concatenate([-jnp.sin(ar), jnp.sin(ar), -jnp.sin(ac), jnp.sin(ac)], axis=-1)
    cos = jnp.concatenate([jnp.ones((Lc, dim), F32), cos], axis=0)
    sin = jnp.concatenate([jnp.zeros((Lc, dim), F32), sin], axis=0)
    rep = LANES // dim
    return jnp.tile(cos, (1, rep)), jnp.tile(sin, (1, rep))


def _rope(x, cos, sin, quarter):
    lane = lax.broadcasted_iota(jnp.int32, x.shape, 1)
    first = (lane % (2 * quarter)) < quarter
    partner = jnp.where(first, pltpu.roll(x, LANES - quarter, 1), pltpu.roll(x, quarter, 1))
    return x * cos + partner * sin


def _inproj_kernel(x_ref, mod_ref, n1g_ref, w_ref, cosa_ref, sina_ref, cosd_ref, sind_ref,
                   gqa_ref, gka_ref, gqd_ref, gkd_ref, lb_ref, g64_ref, g32_ref,
                   qa_ref, ka_ref, va_ref, hqf_ref, hlff_ref, hkf_ref, hqb_ref, hlfb_ref, hkb_ref,
                   hi_ref, hg_ref, qd_ref, kd_ref, vd_ref, p_ref, *, n_batch, n_ctx_tiles):
    b = pl.program_id(0)
    t = pl.program_id(1)
    mrow = jnp.where(t < n_ctx_tiles, n_batch, b)
    m = mod_ref[pl.ds(mrow, 1), :]
    shift, scale = m[:, 0:D_MODEL], m[:, D_MODEL:2 * D_MODEL]

    x = x_ref[0]
    ms = jnp.mean(x * x, axis=-1, keepdims=True)
    h = (x * lax.rsqrt(ms + EPS) * n1g_ref[...]) * (1.0 + scale) + shift
    h = h.astype(BF16)

    edges = sorted(set(_OFF) | {C_AQ + 256})
    issued = [0]

    def proj(c0, width):
        target = min(c0 + width + INPROJ_AHEAD, IN_WIDTH)
        while issued[0] < target:
            e0 = issued[0]
            e1 = min(e for e in edges if e > e0)
            p_ref[:, e0:e1] = jnp.dot(h, w_ref[:, e0:e1], preferred_element_type=F32)
            issued[0] = e1
        return p_ref[:, c0:c0 + width]

    g64 = g64_ref[...]
    g32 = g32_ref[...]
    cosa, sina = cosa_ref[...], sina_ref[...]
    cosd, sind = cosd_ref[...], sind_ref[...]

    for c in range(2):
        p = proj(C_AQ + 256 * c, 256)
        qn = p * lax.rsqrt(_group_sumsq(p, g64) * (1.0 / WIN_HEAD_DIM) + EPS)
        qn = qn * (gqa_ref[:, 256 * c:256 * (c + 1)] * (WIN_HEAD_DIM ** -0.5 * LOG2E))
        for j in range(2):
            r = _rope(qn[:, LANES * j:LANES * (j + 1)], cosa, sina, WIN_HEAD_DIM // 4)
            qa_ref[0, :, 256 * c + LANES * j:256 * c + LANES * (j + 1)] = r.astype(BF16)
    p = proj(C_AK, 128)
    kn = p * lax.rsqrt(_group_sumsq(p, g64[:LANES, :LANES]) * (1.0 / WIN_HEAD_DIM) + EPS) * gka_ref[...]
    ka_ref[0] = _rope(kn, cosa, sina, WIN_HEAD_DIM // 4).astype(BF16)
    va_ref[0] = proj(C_AV, 128).astype(BF16)

    for d, (cq, cf, q_ref, lf_ref, k_ref) in enumerate(((C_BQF, C_BFF, hqf_ref, hlff_ref, hkf_ref),
                                                        (C_BQB, C_BFB, hqb_ref, hlfb_ref, hkb_ref))):
        q_ref[0] = proj(cq, 256)
        lb = lb_ref[d:d + 1, :]
        f = lb + (1.0 - lb) * _sigmoid(proj(cf, 256))
        lf_ref[0] = jnp.log(f)
        k_ref[0] = 1.0 - f
    hi_ref[0] = proj(C_BI, 256).astype(BF16)
    g = proj(C_BG, 256)
    hg_ref[0] = g * _sigmoid(g)

    for c0, gain_ref, o_ref, mult in ((C_CQ, gqd_ref, qd_ref, DIFF_QK_DIM ** -0.5 * LOG2E),
                                      (C_CK, gkd_ref, kd_ref, 1.0)):
        p = proj(c0, 256)
        pn = p * lax.rsqrt(_group_sumsq(p, g32) * (1.0 / DIFF_QK_DIM) + EPS) * (gain_ref[...] * mult)
        for j in range(2):
            r = _rope(pn[:, LANES * j:LANES * (j + 1)], cosd, sind, DIFF_QK_DIM // 4)
            o_ref[0, :, LANES * j:LANES * (j + 1)] = r.astype(BF16)
    vd_ref[0] = proj(C_CV, 256).astype(BF16)


def _inproj(xc, mod_l, n1g, w_in, tabs, gains, lb_l, g64, g32, n_batch, Lc):
    B, Lt, D = xc.shape
    nt = Lt // TOK_TILE
    cosa, sina, cosd, sind = tabs
    gqa, gka, gqd, gkd = gains
    full = lambda a: pl.BlockSpec(a.shape, lambda b, t: (0,) * a.ndim)
    tab = pl.BlockSpec((TOK_TILE, LANES), lambda b, t: (t, 0))
    tok = lambda w: pl.BlockSpec((1, TOK_TILE, w), lambda b, t: (b, t, 0))
    widths = (512, 128, 128, 256, 256, 256, 256, 256, 256, 256, 256, 256, 256, 256)
    dtypes = (BF16, BF16, BF16, F32, F32, F32, F32, F32, F32, BF16, F32, BF16, BF16, BF16)
    return pl.pallas_call(
        functools.partial(_inproj_kernel, n_batch=n_batch, n_ctx_tiles=Lc // TOK_TILE),
        grid=(B, nt),
        in_specs=[tok(D), full(mod_l), full(n1g), full(w_in), tab, tab, tab, tab,
                  full(gqa), full(gka), full(gqd), full(gkd), full(lb_l), full(g64), full(g32)],
        out_specs=[tok(w) for w in widths],
        out_shape=[jax.ShapeDtypeStruct((B, Lt, w), dt) for w, dt in zip(widths, dtypes)],
        scratch_shapes=[pltpu.VMEM((TOK_TILE, IN_WIDTH), F32)],
        compiler_params=_cparams("parallel", "arbitrary"),
        name="inproj",
    )(xc, mod_l, n1g, w_in, cosa, sina, cosd, sind, gqa, gka, gqd, gkd, lb_l, g64, g32)


def _gqa_softmax_pv(s_blocks, v_blocks, sinks):
    rows = s_blocks[0].shape[0]
    per_head = rows // len(sinks)
    rid = lax.broadcasted_iota(jnp.int32, (rows, 1), 0)
    sink = jnp.full((rows, 1), sinks[-1], F32)
    for g in range(len(sinks) - 2, -1, -1):
        sink = jnp.where(rid < (g + 1) * per_head, sinks[g], sink)
    s = jnp.concatenate(s_blocks, axis=1)
    m = jnp.maximum(jnp.max(s, axis=-1, keepdims=True), sink)
    e = jnp.exp2(s - m).astype(BF16)
    pv = jnp.dot(e, jnp.concatenate(v_blocks, axis=0), preferred_element_type=F32)
    den = pv[:, WIN_HEAD_DIM:WIN_HEAD_DIM + 1] + jnp.exp2(sink - m)
    return pv[:, :WIN_HEAD_DIM] / den


def _win_attn_kernel(sink_ref, q_ref, kp_ref, kc_ref, kn_ref, kx_ref, vp_ref, vc_ref, vn_ref, vx_ref, o_ref):
    j = pl.program_id(1)
    nb = pl.num_programs(1)
    group = WIN_HEADS // WIN_KV_HEADS
    rows = group * ATT_BLOCK
    t = lax.broadcasted_iota(jnp.int32, (rows, ATT_BLOCK), 0) % ATT_BLOCK
    s = lax.broadcasted_iota(jnp.int32, (rows, ATT_BLOCK), 1)
    mask_prev = (s >= t) & (j > 0)
    mask_next = (s <= t) & (j < nb - 1)
    def scores(hk):
        q = q_ref[0, hk * group:(hk + 1) * group].reshape(rows, WIN_HEAD_DIM)
        dot = lambda k_ref: jnp.dot(q, k_ref[0, hk], preferred_element_type=F32)
        return [jnp.where(mask_prev, dot(kp_ref), NEG), dot(kc_ref),
                jnp.where(mask_next, dot(kn_ref), NEG), dot(kx_ref)]

    blocks_all = [scores(hk) for hk in range(WIN_KV_HEADS)]
    for hk in range(WIN_KV_HEADS):
        o = _gqa_softmax_pv(blocks_all[hk], [vp_ref[0, hk], vc_ref[0, hk], vn_ref[0, hk], vx_ref[0, hk]],
                            [sink_ref[hk * group + g] for g in range(group)])
        o_ref[0, hk * group:(hk + 1) * group] = o.reshape(group, ATT_BLOCK, WIN_HEAD_DIM).astype(BF16)


def _ctx_attn_kernel(sink_ref, q_ref, kx_ref, vx_ref, o_ref):
    group = WIN_HEADS // WIN_KV_HEADS
    n = q_ref.shape[2]
    for hk in range(WIN_KV_HEADS):
        q = q_ref[0, hk * group:(hk + 1) * group].reshape(group * n, WIN_HEAD_DIM)
        s_ctx = jnp.dot(q, kx_ref[0, hk], preferred_element_type=F32)
        o = _gqa_softmax_pv([s_ctx], [vx_ref[0, hk]], [sink_ref[hk * group + g] for g in range(group)])
        o_ref[0, hk * group:(hk + 1) * group] = o.reshape(group, n, WIN_HEAD_DIM).astype(BF16)


def _win_attention(sink, q_h, kT, v_h, L, Lc, need_ctx):
    B = q_h.shape[0]
    nb = L // ATT_BLOCK
    off = Lc // ATT_BLOCK
    dh = WIN_HEAD_DIM
    smem = pl.BlockSpec(memory_space=pltpu.SMEM)
    kblk = lambda f: pl.BlockSpec((1, WIN_KV_HEADS, dh, ATT_BLOCK), lambda b, j: (b, 0, 0, off + f(j)))
    vblk = lambda f: pl.BlockSpec((1, WIN_KV_HEADS, ATT_BLOCK, LANES), lambda b, j: (b, 0, off + f(j), 0))
    prev = lambda j: jnp.maximum(j - 1, 0)
    cur = lambda j: j
    nxt = lambda j: jnp.minimum(j + 1, nb - 1)
    kctx = pl.BlockSpec((1, WIN_KV_HEADS, dh, Lc), lambda b, j: (b, 0, 0, 0))
    vctx = pl.BlockSpec((1, WIN_KV_HEADS, Lc, LANES), lambda b, j: (b, 0, 0, 0))
    o_lat = pl.pallas_call(
        _win_attn_kernel,
        grid=(B, nb),
        in_specs=[smem, pl.BlockSpec((1, WIN_HEADS, ATT_BLOCK, dh), lambda b, j: (b, 0, off + j, 0)),
                  kblk(prev), kblk(cur), kblk(nxt), kctx, vblk(prev), vblk(cur), vblk(nxt), vctx],
        out_specs=pl.BlockSpec((1, WIN_HEADS, ATT_BLOCK, dh), lambda b, j: (b, 0, j, 0)),
        out_shape=jax.ShapeDtypeStruct((B, WIN_HEADS, L, dh), BF16),
        compiler_params=_cparams("parallel", "arbitrary"),
        name="win_attn",
    )(sink, q_h, kT, kT, kT, kT, v_h, v_h, v_h, v_h)
    if not need_ctx:
        return o_lat, None
    o_ctx = pl.pallas_call(
        _ctx_attn_kernel,
        grid=(B,),
        in_specs=[smem, pl.BlockSpec((1, WIN_HEADS, Lc, dh), lambda b: (b, 0, 0, 0)),
                  pl.BlockSpec((1, WIN_KV_HEADS, dh, Lc), lambda b: (b, 0, 0, 0)),
                  pl.BlockSpec((1, WIN_KV_HEADS, Lc, LANES), lambda b: (b, 0, 0, 0))],
        out_specs=pl.BlockSpec((1, WIN_HEADS, Lc, dh), lambda b: (b, 0, 0, 0)),
        out_shape=jax.ShapeDtypeStruct((B, WIN_HEADS, Lc, dh), BF16),
        compiler_params=_cparams("parallel"),
        name="ctx_attn",
    )(sink, q_h, kT, v_h)
    return o_lat, o_ctx


def _hgrn_direction(q_ref, lf_ref, k_ref, v_ref, vT_ref, g64_ref, o_ref, st_ref, *, reverse):
    @pl.when(pl.program_id(1) == 0)
    def _():
        st_ref[...] = jnp.zeros_like(st_ref)

    width = HG_HEADS * HG_KEY_DIM
    n_pairs = width // LANES
    n_groups = HG_BLOCK // HG_GROUP
    n_sub = HG_GROUP // HG_SUB
    half = HG_SUB // 2
    ri = lax.broadcasted_iota(jnp.int32, (HG_GROUP, HG_GROUP), 0)
    ci = lax.broadcasted_iota(jnp.int32, (HG_GROUP, HG_GROUP), 1)
    in_sub = (ri // HG_SUB) == (ci // HG_SUB)
    tri = (in_sub & ((ci >= ri) if reverse else (ci <= ri))).astype(BF16)
    rows_full = lax.broadcasted_iota(jnp.int32, (HG_SUB, width), 0)
    rows_half = lax.broadcasted_iota(jnp.int32, (half, width), 0) + (0 if reverse else half)
    same_head = (lax.broadcasted_iota(jnp.int32, (LANES, LANES), 0) // HG_VAL_DIM
                 == lax.broadcasted_iota(jnp.int32, (LANES, LANES), 1) // HG_KEY_DIM)
    g64 = g64_ref[...]
    end_row = 0 if reverse else HG_SUB - 1
    half_lo = 0 if reverse else half

    def is_full(s):
        return (s >= half) if reverse else (s < half)

    def group(gi):
        g = (n_groups - 1 - gi) if reverse else gi
        g0 = pl.multiple_of(g * HG_GROUP, HG_GROUP)
        lf = lf_ref[0, pl.ds(g0, HG_GROUP), :]
        q = q_ref[0, pl.ds(g0, HG_GROUP), :]
        k = k_ref[0, pl.ds(g0, HG_GROUP), :]
        vf = v_ref[0, pl.ds(g0, HG_GROUP), :].astype(F32)
        vT = vT_ref[0, :, pl.ds(g0, HG_GROUP)]

        hi, mid, lo = _split3(lf)
        bcum = (jnp.dot(tri, hi, preferred_element_type=F32) + jnp.dot(tri, mid, preferred_element_type=F32)
                + jnp.dot(tri, lo, preferred_element_type=F32))
        qt = (q * jnp.exp(bcum)).astype(BF16)

        ends = [bcum[c * HG_SUB + end_row:c * HG_SUB + end_row + 1] for c in range(n_sub)]
        bend = jnp.concatenate([jnp.broadcast_to(e, (HG_SUB, width)) for e in ends], axis=0)
        khat = (k * jnp.exp(bend - bcum)).astype(BF16)

        def state_update(c, p):
            r0 = c * HG_SUB
            pieces = [khat[r0:r0 + HG_SUB, p * LANES:(p + 1) * LANES]]
            if r0:
                pieces.insert(0, jnp.zeros((r0, LANES), BF16))
            if r0 + HG_SUB < HG_GROUP:
                pieces.append(jnp.zeros((HG_GROUP - r0 - HG_SUB, LANES), BF16))
            return jnp.dot(vT[p * LANES:(p + 1) * LANES, :], jnp.concatenate(pieces, axis=0),
                           preferred_element_type=F32)

        upds = [[state_update(c, p) for p in range(n_pairs)] for c in range(n_sub)]
        sts = [st_ref[p] for p in range(n_pairs)]
        inter = [None] * n_sub
        for c in (range(n_sub - 1, -1, -1) if reverse else range(n_sub)):
            r0 = c * HG_SUB
            dec = jnp.exp(ends[c])
            o_parts = []
            for p in range(n_pairs):
                ls = slice(p * LANES, (p + 1) * LANES)
                st_b = jnp.where(same_head, sts[p], 0.0).astype(BF16)
                o_parts.append(lax.dot_general(qt[r0:r0 + HG_SUB, ls], st_b, (((1,), (1,)), ((), ())),
                                               preferred_element_type=F32))
                sts[p] = sts[p] * dec[:, ls] + upds[c][p]
            inter[c] = jnp.concatenate(o_parts, axis=1)
        for p in range(n_pairs):
            st_ref[p] = sts[p]

        parts = []
        for c in range(n_sub):
            r0 = c * HG_SUB
            bc, qc, kc = bcum[r0:r0 + HG_SUB], q[r0:r0 + HG_SUB], k[r0:r0 + HG_SUB]
            for s in range(HG_SUB):
                if is_full(s):
                    rows, bt, qq = rows_full, bc, qc
                else:
                    rows, bt, qq = rows_half, bc[half_lo:half_lo + half], qc[half_lo:half_lo + half]
                valid = (rows <= s) if reverse else (rows >= s)
                parts.append(jnp.where(valid, qq * jnp.exp(bt - bc[s:s + 1]) * kc[s:s + 1], 0.0))
        a = jnp.dot(jnp.concatenate(parts, axis=0).astype(BF16), g64, preferred_element_type=F32)
        intra = []
        off = 0
        for c in range(n_sub):
            r0 = c * HG_SUB
            o_full = jnp.zeros((HG_SUB, width), F32)
            o_half = jnp.zeros((half, width), F32)
            for s in range(HG_SUB):
                vs = vf[r0 + s:r0 + s + 1]
                if is_full(s):
                    o_full = o_full + a[off:off + HG_SUB] * vs
                    off += HG_SUB
                else:
                    o_half = o_half + a[off:off + half] * vs
                    off += half
            pad = jnp.zeros((half, width), F32)
            intra.append(o_full + jnp.concatenate([o_half, pad] if reverse else [pad, o_half], axis=0))

        for c in range(n_sub):
            o_ref[0, pl.ds(g0 + c * HG_SUB, HG_SUB), :] = intra[c] + inter[c]

    return group


def _hgrn_kernel(qf_ref, lff_ref, kf_ref, vf_ref, vTf_ref, qb_ref, lfb_ref, kb_ref, vb_ref, vTb_ref, g64_ref,
                 of_ref, ob_ref, stf_ref, stb_ref):
    fwd = _hgrn_direction(qf_ref, lff_ref, kf_ref, vf_ref, vTf_ref, g64_ref, of_ref, stf_ref, reverse=False)
    bwd = _hgrn_direction(qb_ref, lfb_ref, kb_ref, vb_ref, vTb_ref, g64_ref, ob_ref, stb_ref, reverse=True)

    def body(gi, carry):
        fwd(gi)
        bwd(gi)
        return carry

    lax.fori_loop(0, HG_BLOCK // HG_GROUP, body, 0)


def _hgrn_scan(qf, lff, kf, qb, lfb, kb, v, vT, g64, Lc):
    B, Lt, W = qf.shape
    nblk = Lt // HG_BLOCK
    nctx = Lc // HG_BLOCK
    rev = lambda i: jnp.where(i < nctx, nctx - 1 - i, nctx + (nblk - 1 - i))
    fspec = pl.BlockSpec((1, HG_BLOCK, W), lambda b, i: (b, i, 0))
    fspec_t = pl.BlockSpec((1, W, HG_BLOCK), lambda b, i: (b, 0, i))
    bspec = pl.BlockSpec((1, HG_BLOCK, W), lambda b, i: (b, rev(i), 0))
    bspec_t = pl.BlockSpec((1, W, HG_BLOCK), lambda b, i: (b, 0, rev(i)))
    state = pltpu.VMEM((W // LANES, LANES, LANES), F32)
    return pl.pallas_call(
        _hgrn_kernel,
        grid=(B, nblk),
        in_specs=[fspec, fspec, fspec, fspec, fspec_t, bspec, bspec, bspec, bspec, bspec_t,
                  pl.BlockSpec(g64.shape, lambda b, i: (0, 0))],
        out_specs=[fspec, bspec],
        out_shape=[jax.ShapeDtypeStruct((B, Lt, W), F32)] * 2,
        scratch_shapes=[state, state],
        compiler_params=_cparams("parallel", "arbitrary"),
        name="hgrn",
    )(qf, lff, kf, v, vT, qb, lfb, kb, v, vT, g64)


def _diff_attn_kernel(lam_ref, qT_ref, k_ref, vT_ref, og_ref, o_ref, *, key_block, n_keys, post_scale):
    nkb = n_keys // key_block
    n_chain = 2 * DIFF_HEADS
    per_half = LANES // DIFF_QK_DIM
    band = lax.broadcasted_iota(jnp.int32, (LANES, DIFF_TQ), 0) // DIFF_QK_DIM
    ws = []
    for j in range(n_chain):
        half = qT_ref[0, (j // per_half) * LANES:(j // per_half + 1) * LANES, :]
        ws.append(jnp.where(band == j % per_half, half, jnp.zeros_like(half)))

    def scores(kb):
        k0 = pl.multiple_of(kb * key_block, key_block)
        kblks = [k_ref[0, pl.ds(k0, key_block), p * LANES:(p + 1) * LANES] for p in range(n_chain // per_half)]
        sTs = [jnp.dot(kblks[j // per_half], ws[j], preferred_element_type=F32) for j in range(n_chain)]
        vTs = [vT_ref[0, h, :, pl.ds(k0, key_block)] for h in range(DIFF_HEADS)]
        return sTs, vTs

    def pv(vTs, eTs):
        return [jnp.dot(vTs[j // 2], eTs[j], preferred_element_type=F32) for j in range(n_chain)]

    acc0 = tuple(jnp.zeros((DIFF_VT_ROWS, DIFF_TQ), F32) for _ in range(n_chain))

    def fixed_shift():
        def body(kb, accs):
            sTs, vTs = scores(kb)
            pvs = pv(vTs, [jnp.exp2(sTs[j]).astype(BF16) for j in range(n_chain)])
            return tuple(accs[j] + pvs[j] for j in range(n_chain))
        return lax.fori_loop(0, nkb, body, acc0)

    def running_max():
        def body(kb, carry):
            ms, accs = carry
            sTs, vTs = scores(kb)
            m_new = [jnp.maximum(ms[j], jnp.max(sTs[j], axis=0, keepdims=True)) for j in range(n_chain)]
            pvs = pv(vTs, [jnp.exp2(sTs[j] - m_new[j]).astype(BF16) for j in range(n_chain)])
            accs = tuple(accs[j] * jnp.exp2(ms[j] - m_new[j]) + pvs[j] for j in range(n_chain))
            return tuple(m_new), accs
        m0 = tuple(jnp.full((1, DIFF_TQ), NEG, F32) for _ in range(n_chain))
        return lax.fori_loop(0, nkb, body, (m0, acc0))[1]

    accs = lax.cond(lam_ref[1] <= DIFF_MAX_SHIFT, fixed_shift, running_max)
    lam = lam_ref[0]
    for h in range(DIFF_HEADS):
        a0, a1 = accs[2 * h], accs[2 * h + 1]
        o = (a0[:DIFF_V_DIM] / a0[DIFF_V_DIM:DIFF_V_DIM + 1]
             - lam * (a1[:DIFF_V_DIM] / a1[DIFF_V_DIM:DIFF_V_DIM + 1]))
        ms = jnp.mean(o * o, axis=0, keepdims=True)
        o_ref[0, h] = (o * lax.rsqrt(ms + EPS) * (og_ref[...] * post_scale)).astype(BF16)


def _pick_key_block(n):
    best = LANES
    for kb in range(LANES, min(n, DIFF_MAX_KEY_BLOCK) + 1, LANES):
        if n % kb == 0:
            best = kb
    return best


def _diff_attention(lam, qT, k, vT_aug, og_col, q_off, n_q, n_keys, post_scale):
    B = qT.shape[0]
    width = 2 * DIFF_HEADS * DIFF_QK_DIM
    nq = n_q // DIFF_TQ
    qo = q_off // DIFF_TQ
    n_chain = 2 * DIFF_HEADS
    return pl.pallas_call(
        functools.partial(_diff_attn_kernel, key_block=_pick_key_block(n_keys), n_keys=n_keys,
                          post_scale=post_scale),
        grid=(B, nq),
        in_specs=[pl.BlockSpec(memory_space=pltpu.SMEM),
                  pl.BlockSpec((1, width, DIFF_TQ), lambda b, i: (b, 0, qo + i)),
                  pl.BlockSpec((1, n_keys, width), lambda b, i: (b, 0, 0)),
                  pl.BlockSpec((1, DIFF_HEADS, DIFF_VT_ROWS, n_keys), lambda b, i: (b, 0, 0, 0)),
                  pl.BlockSpec((DIFF_V_DIM, 1), lambda b, i: (0, 0))],
        out_specs=pl.BlockSpec((1, DIFF_HEADS, DIFF_V_DIM, DIFF_TQ), lambda b, i: (b, 0, 0, i)),
        out_shape=jax.ShapeDtypeStruct((B, DIFF_HEADS, DIFF_V_DIM, n_q), BF16),
        compiler_params=_cparams("parallel", "arbitrary"),
        name="diff_attn",
    )(lam, qT, k, vT_aug, og_col)


def _outproj_kernel(x_ref, oa_ref, of_ref, ob_ref, gate_ref, od_ref, mod_ref, w_ref, hog_ref, n2g_ref, g64_ref,
                    xm_ref, h2_ref, *, n_batch, n_ctx_tiles, tile_off):
    b = pl.program_id(0)
    t = pl.program_id(1) + tile_off
    mrow = jnp.where(t < n_ctx_tiles, n_batch, b)
    m = mod_ref[pl.ds(mrow, 1), :]
    gate1 = m[:, 2 * D_MODEL:3 * D_MODEL]
    shift2, scale2 = m[:, 3 * D_MODEL:4 * D_MODEL], m[:, 4 * D_MODEL:5 * D_MODEL]

    o = of_ref[0] + ob_ref[0]
    ss = _group_sumsq(o, g64_ref[...])
    wa, wb = WIN_HEADS * WIN_HEAD_DIM, HG_HEADS * HG_VAL_DIM
    mix = jnp.dot(oa_ref[0], w_ref[0:wa, :], preferred_element_type=F32)
    mix += jnp.dot(od_ref[0], w_ref[wa + wb:, :], preferred_element_type=F32)
    on = o * lax.rsqrt(ss * (1.0 / HG_VAL_DIM) + EPS) * hog_ref[...]
    ob = (on * gate_ref[0]).astype(BF16)
    mix += jnp.dot(ob, w_ref[wa:wa + wb, :], preferred_element_type=F32)

    x = x_ref[0] + gate1 * mix
    xm_ref[0] = x
    ms = jnp.mean(x * x, axis=-1, keepdims=True)
    h = (x * lax.rsqrt(ms + EPS) * n2g_ref[...]) * (1.0 + scale2) + shift2
    h2_ref[0] = h.astype(BF16)


def _outproj(xc, oa, of, ob, gate, od, mod_l, w_out, hog, n2g, g64, n_batch, Lc, tile_off, mix_off):
    B, Lt, D = xc.shape
    nt = Lt // TOK_TILE - tile_off
    full = lambda a: pl.BlockSpec(a.shape, lambda b, t: (0,) * a.ndim)
    comb = lambda w: pl.BlockSpec((1, TOK_TILE, w), lambda b, t: (b, t + tile_off, 0))
    part = lambda w: pl.BlockSpec((1, TOK_TILE, w), lambda b, t: (b, t + tile_off - mix_off, 0))
    outs = pl.BlockSpec((1, TOK_TILE, D), lambda b, t: (b, t, 0))
    return pl.pallas_call(
        functools.partial(_outproj_kernel, n_batch=n_batch, n_ctx_tiles=Lc // TOK_TILE, tile_off=tile_off),
        grid=(B, nt),
        in_specs=[comb(D), part(oa.shape[-1]), comb(256), comb(256), comb(256), part(od.shape[-1]),
                  full(mod_l), full(w_out), full(hog), full(n2g), full(g64)],
        out_specs=[outs, outs],
        out_shape=[jax.ShapeDtypeStruct((B, nt * TOK_TILE, D), F32),
                   jax.ShapeDtypeStruct((B, nt * TOK_TILE, D), BF16)],
        compiler_params=_cparams("parallel", "arbitrary"),
        name="outproj",
    )(xc, oa, of, ob, gate, od, mod_l, w_out, hog, n2g, g64)


def _ffn_kernel(x_ref, h_ref, hp_ref, hn_ref, mod_ref, wu_ref, cw_ref, cb_ref, wd_ref, o_ref, u_ref,
                *, n_batch, n_ctx_tiles, tile_off, n_tiles_total):
    b = pl.program_id(0)
    t = pl.program_id(1) + tile_off
    mrow = jnp.where(t < n_ctx_tiles, n_batch, b)
    gate2 = mod_ref[pl.ds(mrow, 1), :][:, 5 * D_MODEL:6 * D_MODEL]

    prev_ok = jnp.logical_and(t != 0, t != n_ctx_tiles)
    next_ok = jnp.logical_and(t != n_ctx_tiles - 1, t != n_tiles_total - 1)
    hp = jnp.where(prev_ok, hp_ref[0], jnp.zeros_like(hp_ref[0]))
    hn = jnp.where(next_ok, hn_ref[0], jnp.zeros_like(hn_ref[0]))
    hcat = jnp.concatenate([hp, h_ref[0], hn], axis=0)

    lo = BF16_ROWS
    n_chunks = D_FF // FF_CHUNK

    def up_project(c):
        for base in (0, D_FF):
            cols = slice(base + c * FF_CHUNK, base + (c + 1) * FF_CHUNK)
            u_ref[:, cols] = jnp.dot(hcat, wu_ref[:, cols], preferred_element_type=F32)

    def conv(c, base):
        cols = slice(base + c * FF_CHUNK, base + (c + 1) * FF_CHUNK)
        w = cw_ref[:, cols]
        return (cb_ref[:, cols] + u_ref[lo - 1:lo - 1 + TOK_TILE, cols] * w[0:1]
                + u_ref[lo:lo + TOK_TILE, cols] * w[1:2] + u_ref[lo + 1:lo + 1 + TOK_TILE, cols] * w[2:3])

    acc = jnp.zeros((TOK_TILE, D_MODEL), F32)
    ahead = FF_AHEAD
    for c in range(min(ahead, n_chunks)):
        up_project(c)
    for c in range(n_chunks):
        if c + ahead < n_chunks:
            up_project(c + ahead)
        a, val = conv(c, 0), conv(c, D_FF)
        act = (a * _sigmoid(a) * val).astype(BF16)
        acc += jnp.dot(act, wd_ref[c * FF_CHUNK:(c + 1) * FF_CHUNK, :], preferred_element_type=F32)
    o_ref[0] = x_ref[0] + gate2 * acc


def _conv_ffn(xm, h2, mod_l, w_up, conv_w, conv_b, w_down, n_batch, Lc, tile_off, n_tiles_total):
    B, T, D = xm.shape
    nt = T // TOK_TILE
    r = TOK_TILE // BF16_ROWS
    last = T // BF16_ROWS - 1
    full = lambda a: pl.BlockSpec(a.shape, lambda b, t: (0,) * a.ndim)
    tok = pl.BlockSpec((1, TOK_TILE, D), lambda b, t: (b, t, 0))
    halo_prev = pl.BlockSpec((1, BF16_ROWS, D), lambda b, t: (b, jnp.maximum(t * r - 1, 0), 0))
    halo_next = pl.BlockSpec((1, BF16_ROWS, D), lambda b, t: (b, jnp.minimum((t + 1) * r, last), 0))
    return pl.pallas_call(
        functools.partial(_ffn_kernel, n_batch=n_batch, n_ctx_tiles=Lc // TOK_TILE, tile_off=tile_off,
                          n_tiles_total=n_tiles_total),
        grid=(B, nt),
        in_specs=[tok, tok, halo_prev, halo_next, full(mod_l), full(w_up), full(conv_w), full(conv_b),
                  full(w_down)],
        out_specs=tok,
        out_shape=jax.ShapeDtypeStruct((B, T, D), F32),
        scratch_shapes=[pltpu.VMEM((TOK_TILE + 2 * BF16_ROWS, 2 * D_FF), F32)],
        compiler_params=_cparams("parallel", "arbitrary"),
        name="conv_ffn",
    )(xm, h2, h2, h2, mod_l, w_up, conv_w, conv_b, w_down)


def _with_ones_column(v):
    pad = jnp.zeros(v.shape[:-1] + (LANES - v.shape[-1],), v.dtype).at[..., 0].set(1.0)
    return jnp.concatenate([v, pad], axis=-1)


def kernel(x, c, ctx, c_ctx, w_mod, b_mod, norm1_g, norm2_g, w_in, win_qnorm_g, win_knorm_g, win_sink,
           hg_lower, hg_onorm_g, diff_qnorm_g, diff_knorm_g, diff_lambda, diff_onorm_g, w_out,
           w_up, conv_w, conv_b, w_down):
    B, L, D = x.shape
    Lc = ctx.shape[1]
    depth = w_mod.shape[0]
    Lt = Lc + L
    assert D == D_MODEL and L % GRID_W == 0
    assert Lc % TOK_TILE == 0 and L % TOK_TILE == 0 and TOK_TILE % HG_BLOCK == 0 and TOK_TILE % ATT_BLOCK == 0
    assert B + 1 <= SUBLANES

    vecs = jnp.zeros((SUBLANES, D), F32).at[:B].set(c.astype(F32)).at[B].set(c_ctx.astype(F32))
    mod = _adaln(vecs, w_mod.astype(F32), b_mod.astype(F32))
    lower, lam_raw = _prep_params(hg_lower, diff_lambda)

    tabs = _rope_tables(L, Lc, WIN_HEAD_DIM) + _rope_tables(L, Lc, DIFF_QK_DIM)
    g64 = _block_diag_ones(256, 64)
    g32 = _block_diag_ones(256, 32)
    row = lambda v, reps: jnp.tile(v.astype(F32), reps).reshape(1, -1)

    xc = jnp.concatenate([ctx.astype(F32), x.astype(F32)], axis=1)
    n_tiles_total = Lt // TOK_TILE
    nct = Lc // TOK_TILE

    for l in range(depth):
        need_ctx = l < depth - 1
        lam_init = 0.8 - 0.6 * math.exp(-0.3 * l)
        gains = (row(win_qnorm_g[l], WIN_HEADS), row(win_knorm_g[l], WIN_KV_HEADS),
                 row(diff_qnorm_g[l], 2 * DIFF_HEADS), row(diff_knorm_g[l], 2 * DIFF_HEADS))
        (qa, ka, va, hqf, hlff, hkf, hqb, hlfb, hkb, hi, hgate, qd, kd, vd) = _inproj(
            xc, mod[l], norm1_g[l].reshape(1, D).astype(F32), w_in[l].astype(BF16), tabs, gains, lower[l],
            g64, g32, B, Lc)

        qa_h = qa.reshape(B, Lt, WIN_HEADS, WIN_HEAD_DIM).transpose(0, 2, 1, 3)
        kaT = ka.reshape(B, Lt, WIN_KV_HEADS, WIN_HEAD_DIM).transpose(0, 2, 3, 1)
        va_h = _with_ones_column(va.reshape(B, Lt, WIN_KV_HEADS, WIN_HEAD_DIM).transpose(0, 2, 1, 3))
        sink = win_sink[l].astype(F32) * LOG2E
        oa_lat, oa_ctx = _win_attention(sink, qa_h, kaT, va_h, L, Lc, need_ctx)

        hiT = hi.transpose(0, 2, 1)
        of, ob = _hgrn_scan(hqf, hlff, hkf, hqb, hlfb, hkb, hi, hiT, g64, Lc)

        qdT = qd.transpose(0, 2, 1)
        vdT = vd.reshape(B, Lt, DIFF_HEADS, DIFF_V_DIM).transpose(0, 2, 3, 1)
        fill = jnp.zeros((B, DIFF_HEADS, DIFF_VT_ROWS - DIFF_V_DIM, Lt), BF16).at[:, :, 0].set(1.0)
        vdT_aug = jnp.concatenate([vdT, fill], axis=2)
        score_bound = (DIFF_QK_DIM * (DIFF_QK_DIM ** -0.5 * LOG2E) * (1.0 + 2.0 ** -6)
                       * jnp.max(jnp.abs(diff_qnorm_g[l].astype(F32))) * jnp.max(jnp.abs(diff_knorm_g[l].astype(F32))))
        lam = jnp.concatenate([lam_raw[l, 0, :1] + lam_init, score_bound.reshape(1)])
        dog = diff_onorm_g[l].reshape(DIFF_V_DIM, 1).astype(F32)
        od_lat = _diff_attention(lam, qdT, kd, vdT_aug, dog, Lc, L, Lt, 1.0 - lam_init)

        def token_major(o):
            return o.transpose(0, 2, 1, 3).reshape(B, o.shape[2], -1)

        def token_major_t(oT):
            return oT.transpose(0, 3, 1, 2).reshape(B, oT.shape[3], -1)

        if need_ctx:
            od_ctx = _diff_attention(lam, qdT, kd, vdT_aug, dog, 0, Lc, Lc, 1.0 - lam_init)
            oa = jnp.concatenate([token_major(oa_ctx), token_major(oa_lat)], axis=1)
            od = jnp.concatenate([token_major_t(od_ctx), token_major_t(od_lat)], axis=1)
            tile_off, mix_off = 0, 0
        else:
            oa, od = token_major(oa_lat), token_major_t(od_lat)
            tile_off, mix_off = nct, nct

        xm, h2 = _outproj(xc, oa, of, ob, hgate, od, mod[l], w_out[l].astype(BF16),
                          row(hg_onorm_g[l], HG_HEADS), norm2_g[l].reshape(1, D).astype(F32), g64,
                          B, Lc, tile_off, mix_off)
        xc = _conv_ffn(xm, h2, mod[l], w_up[l].astype(BF16), conv_w[l].astype(F32),
                       conv_b[l].reshape(1, -1).astype(F32), w_down[l].astype(BF16),
                       B, Lc, tile_off, n_tiles_total)

    return xc
```
